```python
import jax, jax.numpy as jnp
from jax import lax
import numpy as np

D_MODEL = 1024
BATCH = 4
SEQ = 8192
DEPTH = 4

CHUNK = 64
N_EVEN = (DEPTH + 1) // 2
N_ODD = DEPTH // 2
EPS = 1e-6

LRU_WIDTH = D_MODEL // 2
LRU_HEADS = 8
LRU_HEAD_DIM = LRU_WIDTH // LRU_HEADS
LRU_CONV = 4
LRU_C = 8.0
SCONV_WIDTH = D_MODEL // 2
SCONV_K = 3
EVEN_IN = 2 * LRU_WIDTH + 3 * SCONV_WIDTH
SPLIT_EVEN = (LRU_WIDTH, 2 * LRU_WIDTH, 2 * LRU_WIDTH + SCONV_WIDTH, 2 * LRU_WIDTH + 2 * SCONV_WIDTH)
EVEN_MIX = LRU_WIDTH + SCONV_WIDTH

POOL_WIDTH = D_MODEL // 2
POOL_WINDOWS = (2, 4, 8, 16)
POOL_GROUPS = len(POOL_WINDOWS)
POOL_GROUP = POOL_WIDTH // POOL_GROUPS
HGRN_WIDTH = D_MODEL // 2
HGRN_HEADS = 4
HGRN_DK = HGRN_WIDTH // HGRN_HEADS
HGRN_DV = HGRN_WIDTH // HGRN_HEADS
ODD_IN = POOL_WIDTH + 4 * HGRN_WIDTH
SPLIT_ODD = (POOL_WIDTH, POOL_WIDTH + HGRN_WIDTH, POOL_WIDTH + 2 * HGRN_WIDTH, POOL_WIDTH + 3 * HGRN_WIDTH)
ODD_MIX = POOL_WIDTH + HGRN_WIDTH

FFN_DIM = 256 * ((8 * D_MODEL // 3 + 255) // 256)
FFN_K = 3

kernel_name = "hybrid_rglru_shortconv_pool_hgrn2_convffn"


def rmsnorm(x, g):
    xf = x.astype(jnp.float32)
    y = xf * lax.rsqrt(jnp.mean(xf * xf, axis=-1, keepdims=True) + EPS)
    return (y * g.astype(jnp.float32)).astype(x.dtype)


def causal_dwconv(u, w):
    K = w.shape[0]
    S = u.shape[1]
    up = jnp.pad(u, ((0, 0), (K - 1, 0), (0, 0)))
    y = up[:, K - 1:K - 1 + S] * w[K - 1]
    for j in range(K - 1):
        y = y + up[:, j:j + S] * w[j]
    return y


def rg_lru(xc, w_a, b_a, w_i, b_i, lam):
    f32 = jnp.float32
    Bsz, S, _ = xc.shape
    xf = xc.astype(f32)
    xh = xf.reshape(Bsz, S, LRU_HEADS, LRU_HEAD_DIM)
    r = jax.nn.sigmoid(jnp.einsum('bshi,hij->bshj', xh, w_a.astype(f32)).reshape(Bsz, S, LRU_WIDTH) + b_a.astype(f32))
    ig = jax.nn.sigmoid(jnp.einsum('bshi,hij->bshj', xh, w_i.astype(f32)).reshape(Bsz, S, LRU_WIDTH) + b_i.astype(f32))
    log_a = -LRU_C * r * jax.nn.softplus(-lam.astype(f32))
    mult = jnp.sqrt(-jnp.expm1(2.0 * log_a))
    mult = jnp.where(jnp.arange(S)[None, :, None] == 0, 1.0, mult)
    hb = mult * ig * xf

    def combine(left, right):
        a_l, h_l = left
        a_r, h_r = right
        return a_l * a_r, a_r * h_l + h_r

    _, h = lax.associative_scan(combine, (jnp.exp(log_a), hb), axis=1)
    return h


def pool_mixer(u, w_grp, scale):
    f32 = jnp.float32
    Bsz, S, _ = u.shape
    uf = u.astype(f32)
    cs = jnp.cumsum(uf, axis=1)
    t1 = jnp.arange(1, S + 1, dtype=f32)[None, :, None]
    outs = []
    for gi, win in enumerate(POOL_WINDOWS):
        sl = slice(gi * POOL_GROUP, (gi + 1) * POOL_GROUP)
        c = cs[..., sl]
        lag = jnp.pad(c, ((0, 0), (win, 0), (0, 0)))[:, :S]
        outs.append((c - lag) / jnp.minimum(t1, float(win)) - uf[..., sl])
    p = jnp.stack(outs, axis=2)
    y = jnp.einsum('bsgi,gio->bsgo', p, w_grp.astype(f32)).reshape(Bsz, S, POOL_WIDTH)
    return (y * scale.astype(f32)).astype(u.dtype)


def hgrn2(q, fz, v, g, lb, norm_g):
    f32 = jnp.float32
    Bsz, S, _ = q.shape
    nC = S // CHUNK
    fz = fz.astype(f32)
    lb = lb.astype(f32)
    log_f = jnp.logaddexp(jnp.log(lb), jnp.log1p(-lb) + jax.nn.log_sigmoid(fz))
    k = (1.0 - lb) * jax.nn.sigmoid(-fz)

    def chunked(a, d):
        return a.astype(f32).reshape(Bsz, nC, CHUNK, HGRN_HEADS, d)

    qc = chunked(q, HGRN_DK)
    kc = chunked(k, HGRN_DK)
    lfc = chunked(log_f, HGRN_DK)
    vc = chunked(v, HGRN_DV)

    def intra_step(s, inp):
        lf_t, k_t, v_t, q_t = inp
        s = jnp.exp(lf_t)[..., None] * s + k_t[..., None] * v_t[..., None, :]
        return s, jnp.einsum('bchk,bchkv->bchv', q_t, s)

    tm = lambda a: jnp.moveaxis(a, 2, 0)
    s0 = jnp.zeros((Bsz, nC, HGRN_HEADS, HGRN_DK, HGRN_DV), f32)
    ds, o_intra = lax.scan(intra_step, s0, (tm(lfc), tm(kc), tm(vc), tm(qc)))
    o_intra = jnp.moveaxis(o_intra, 0, 2)

    G = jnp.cumsum(lfc, axis=2)

    def inter_step(s, inp):
        dec, ds_c = inp
        return jnp.exp(dec)[..., None] * s + ds_c, s

    _, s_prev = lax.scan(inter_step, jnp.zeros((Bsz, HGRN_HEADS, HGRN_DK, HGRN_DV), f32),
                         (jnp.moveaxis(G[:, :, -1], 1, 0), jnp.moveaxis(ds, 1, 0)))
    o_inter = jnp.einsum('bclhk,cbhkv->bclhv', qc * jnp.exp(G), s_prev)
    o = (o_intra + o_inter).reshape(Bsz, S, HGRN_HEADS, HGRN_DV)
    o = o * lax.rsqrt(jnp.mean(o * o, axis=-1, keepdims=True) + EPS) * norm_g.astype(f32).reshape(HGRN_HEADS, HGRN_DV)
    o = o.reshape(Bsz, S, HGRN_WIDTH) * jax.nn.silu(g.astype(f32))
    return o.astype(q.dtype)


def setup_inputs(seed: int = 0) -> dict:
    key = jax.random.key(seed)
    ks = iter(jax.random.split(key, 32))
    f32 = jnp.float32

    def nrm(shape, s):
        return jax.random.normal(next(ks), shape, f32) * s

    x = nrm((BATCH, SEQ, D_MODEL), 1.0)
    g_mix = 1.0 + nrm((DEPTH, D_MODEL), 0.02)
    g_ffn = 1.0 + nrm((DEPTH, D_MODEL), 0.02)
    g_final = 1.0 + nrm((D_MODEL,), 0.02)
    w_in_even = nrm((N_EVEN, D_MODEL, EVEN_IN), D_MODEL ** -0.5)
    w_out_even = nrm((N_EVEN, EVEN_MIX, D_MODEL), EVEN_MIX ** -0.5)
    lru_conv_w = nrm((N_EVEN, LRU_CONV, LRU_WIDTH), LRU_CONV ** -0.5)
    lru_conv_b = nrm((N_EVEN, LRU_WIDTH), 0.02)
    lru_wa = nrm((N_EVEN, LRU_HEADS, LRU_HEAD_DIM, LRU_HEAD_DIM), LRU_HEAD_DIM ** -0.5)
    lru_ba = nrm((N_EVEN, LRU_WIDTH), 0.02)
    lru_wi = nrm((N_EVEN, LRU_HEADS, LRU_HEAD_DIM, LRU_HEAD_DIM), LRU_HEAD_DIM ** -0.5)
    lru_bi = nrm((N_EVEN, LRU_WIDTH), 0.02)
    a_c = jax.random.uniform(next(ks), (N_EVEN, LRU_WIDTH), f32, minval=0.9, maxval=0.999)
    s_a = a_c ** (1.0 / LRU_C)
    lru_lambda = jnp.log(s_a) - jnp.log1p(-s_a)
    sconv_w = nrm((N_EVEN, SCONV_K, SCONV_WIDTH), SCONV_K ** -0.5)
    w_in_odd = nrm((N_ODD, D_MODEL, ODD_IN), D_MODEL ** -0.5)
    w_out_odd = nrm((N_ODD, ODD_MIX, D_MODEL), ODD_MIX ** -0.5)
    pool_w = nrm((N_ODD, POOL_GROUPS, POOL_GROUP, POOL_GROUP), POOL_GROUP ** -0.5)
    pool_scale = 1.0 + nrm((N_ODD, POOL_WIDTH), 0.02)
    hgrn_lb_logits = nrm((N_ODD, HGRN_WIDTH), 0.5)
    hgrn_norm_g = 1.0 + nrm((N_ODD, HGRN_WIDTH), 0.02)
    ffn_w_up = nrm((DEPTH, D_MODEL, FFN_DIM), D_MODEL ** -0.5)
    ffn_w_gate = nrm((DEPTH, D_MODEL, FFN_DIM), D_MODEL ** -0.5)
    ffn_conv_w = nrm((DEPTH, FFN_K, FFN_DIM), FFN_K ** -0.5)
    ffn_conv_b = nrm((DEPTH, FFN_DIM), 0.02)
    ffn_w_down = nrm((DEPTH, FFN_DIM, D_MODEL), FFN_DIM ** -0.5)
    return {"x": x, "g_mix": g_mix, "g_ffn": g_ffn, "g_final": g_final,
            "w_in_even": w_in_even, "w_out_even": w_out_even, "lru_conv_w": lru_conv_w, "lru_conv_b": lru_conv_b,
            "lru_wa": lru_wa, "lru_ba": lru_ba, "lru_wi": lru_wi, "lru_bi": lru_bi, "lru_lambda": lru_lambda,
            "sconv_w": sconv_w, "w_in_odd": w_in_odd, "w_out_odd": w_out_odd, "pool_w": pool_w,
            "pool_scale": pool_scale, "hgrn_lb_logits": hgrn_lb_logits, "hgrn_norm_g": hgrn_norm_g,
            "ffn_w_up": ffn_w_up, "ffn_w_gate": ffn_w_gate, "ffn_conv_w": ffn_conv_w, "ffn_conv_b": ffn_conv_b,
            "ffn_w_down": ffn_w_down}


def reference(x, g_mix, g_ffn, g_final, w_in_even, w_out_even, lru_conv_w, lru_conv_b, lru_wa, lru_ba,
              lru_wi, lru_bi, lru_lambda, sconv_w, w_in_odd, w_out_odd, pool_w, pool_scale,
              hgrn_lb_logits, hgrn_norm_g, ffn_w_up, ffn_w_gate, ffn_conv_w, ffn_conv_b, ffn_w_down):
    f32 = jnp.float32
    lb_all = jnp.cumsum(jax.nn.softmax(hgrn_lb_logits.astype(f32), axis=0), axis=0)
    lb_all = lb_all - lb_all[0]
    for l in range(DEPTH):
        h = rmsnorm(x, g_mix[l])
        if l % 2 == 0:
            e = l // 2
            z = h @ w_in_even[e]
            xa, ga, hb, bg, cg = jnp.split(z, SPLIT_EVEN, axis=-1)
            xa = causal_dwconv(xa, lru_conv_w[e]) + lru_conv_b[e]
            ya = (rg_lru(xa, lru_wa[e], lru_ba[e], lru_wi[e], lru_bi[e], lru_lambda[e])
                  * jax.nn.gelu(ga.astype(f32))).astype(x.dtype)
            yb = bg * causal_dwconv(cg * hb, sconv_w[e])
            x = x + jnp.concatenate([ya, yb], axis=-1) @ w_out_even[e]
        else:
            o = l // 2
            z = h @ w_in_odd[o]
            uc, q, fz, iv, gd = jnp.split(z, SPLIT_ODD, axis=-1)
            yc = pool_mixer(uc, pool_w[o], pool_scale[o])
            yd = hgrn2(q, fz, iv, gd, lb_all[o], hgrn_norm_g[o])
            x = x + jnp.concatenate([yc, yd], axis=-1) @ w_out_odd[o]
        h = rmsnorm(x, g_ffn[l])
        u = causal_dwconv(h @ ffn_w_up[l], ffn_conv_w[l]) + ffn_conv_b[l]
        x = x + (jax.nn.gelu(u) * (h @ ffn_w_gate[l])) @ ffn_w_down[l]
    return rmsnorm(x, g_final)
```

```python
import functools

import jax
import jax.numpy as jnp
import numpy as np
from jax import lax
from jax.experimental import pallas as pl
from jax.experimental.pallas import tpu as pltpu

F32 = jnp.float32
BF16 = jnp.bfloat16

SUBLANES = 8
LANES = 128
MXU_DIM = 256
VMEM_LIMIT_BYTES = 56 * 1024 * 1024

EPS = 1e-6
LRU_C = 8.0
POOL_WINDOWS = (2, 4, 8, 16)

SEQ_TILE = 512
SEG = SEQ_TILE // SUBLANES
ROW_CHUNK = 64
FFN_CHUNK = 256
HGRN_LEVELS = SEG.bit_length()


def _cparams():
    return pltpu.CompilerParams(
        dimension_semantics=("arbitrary", "arbitrary"),
        vmem_limit_bytes=VMEM_LIMIT_BYTES,
    )


def _sigmoid_pair(v):
    e = jnp.exp(-jnp.abs(v))
    r = 1.0 / (1.0 + e)
    er = e * r
    pos = v >= 0
    return jnp.where(pos, r, er), jnp.where(pos, er, r)


def _sigmoid(v):
    return _sigmoid_pair(v)[0]


def _rmsnorm_rows(x, g):
    ms = jnp.mean(x * x, axis=-1, keepdims=True)
    return x * lax.rsqrt(ms + EPS) * g


def _rmsnorm_to(x_ref, g_ref, h_ref):
    def body(c, carry):
        r0 = pl.multiple_of(c * ROW_CHUNK, ROW_CHUNK)
        x = x_ref[pl.ds(r0, ROW_CHUNK), :]
        h_ref[pl.ds(r0, ROW_CHUNK), :] = _rmsnorm_rows(x, g_ref[...]).astype(h_ref.dtype)
        return carry
    lax.fori_loop(0, SEQ_TILE // ROW_CHUNK, body, 0)


def _fill_history(buf_ref, carry_ref, groups, lanes):
    for g in range(groups):
        tail = buf_ref[pl.ds(SEQ_TILE + g * SUBLANES, SUBLANES), lanes]
        rolled = pltpu.roll(tail, 1, axis=0)
        sub = lax.broadcasted_iota(jnp.int32, rolled.shape, 0)
        prev = carry_ref[pl.ds(g * SUBLANES, SUBLANES), lanes]
        buf_ref[pl.ds(g * SUBLANES, SUBLANES), lanes] = jnp.where(sub == 0, prev, rolled)
        carry_ref[pl.ds(g * SUBLANES, SUBLANES), lanes] = rolled


def _row_time(j, i0, shape):
    row = lax.broadcasted_iota(jnp.int32, shape, 0)
    sub = row & (SUBLANES - 1)
    grp = row >> (SUBLANES.bit_length() - 1)
    return j * SEQ_TILE + sub * SEG + grp + i0


def _ffn_kernel(x_ref, g_ref, wup_ref, wgate_ref, cw_ref, cb_ref, wdown_ref, gfin_ref, o_ref,
                h_ref, up_ref, gate_ref, act_ref, acc_ref, carry_ref, *, final_norm):
    hist = 2
    hrows = hist * SUBLANES
    ffn_dim = wup_ref.shape[1]

    @pl.when(pl.program_id(1) == 0)
    def _():
        carry_ref[...] = jnp.zeros_like(carry_ref)

    _rmsnorm_to(x_ref, g_ref, h_ref)

    for c in range(ffn_dim // FFN_CHUNK):
        cols = pl.ds(c * FFN_CHUNK, FFN_CHUNK)
        up_ref[pl.ds(hrows, SEQ_TILE), :] = jnp.dot(
            h_ref[...], wup_ref[:, cols], preferred_element_type=F32)
        gate_ref[...] = jnp.dot(h_ref[...], wgate_ref[:, cols], preferred_element_type=F32)
        for g in range(hist):
            tail = up_ref[pl.ds(SEQ_TILE + g * SUBLANES, SUBLANES), :]
            rolled = pltpu.roll(tail, 1, axis=0)
            sub = lax.broadcasted_iota(jnp.int32, rolled.shape, 0)
            prev = carry_ref[pl.ds(g * SUBLANES, SUBLANES), cols]
            up_ref[pl.ds(g * SUBLANES, SUBLANES), :] = jnp.where(sub == 0, prev, rolled)
            carry_ref[pl.ds(g * SUBLANES, SUBLANES), cols] = rolled
        w0 = cw_ref[0:1, cols]
        w1 = cw_ref[1:2, cols]
        w2 = cw_ref[2:3, cols]
        b = cb_ref[:, cols]
        for r in range(SEQ_TILE // ROW_CHUNK):
            r0 = r * ROW_CHUNK
            u = (up_ref[pl.ds(r0 + hrows, ROW_CHUNK), :] * w2
                 + up_ref[pl.ds(r0 + hrows - SUBLANES, ROW_CHUNK), :] * w1
                 + up_ref[pl.ds(r0, ROW_CHUNK), :] * w0 + b)
            a = jax.nn.gelu(u) * gate_ref[pl.ds(r0, ROW_CHUNK), :]
            act_ref[pl.ds(r0, ROW_CHUNK), :] = a.astype(BF16)
        down = jnp.dot(act_ref[...], wdown_ref[cols, :], preferred_element_type=F32)
        if c == 0:
            acc_ref[...] = down
        else:
            acc_ref[...] += down

    def out_body(c, carry):
        r0 = pl.multiple_of(c * ROW_CHUNK, ROW_CHUNK)
        y = x_ref[pl.ds(r0, ROW_CHUNK), :] + acc_ref[pl.ds(r0, ROW_CHUNK), :]
        if final_norm:
            y = _rmsnorm_rows(y, gfin_ref[...])
        o_ref[pl.ds(r0, ROW_CHUNK), :] = y
        return carry
    lax.fori_loop(0, SEQ_TILE // ROW_CHUNK, out_body, 0)


def _ffn_call(x, g, wup, wgate, cw, cb, wdown, gfin, final_norm):
    bsz, seq, d = x.shape
    f = wup.shape[1]
    full = lambda a: pl.BlockSpec(a.shape, lambda b, j: (0,) * a.ndim)
    xspec = pl.BlockSpec((None, SEQ_TILE, d), lambda b, j: (b, j, 0))
    return pl.pallas_call(
        functools.partial(_ffn_kernel, final_norm=final_norm),
        out_shape=jax.ShapeDtypeStruct(x.shape, x.dtype),
        grid=(bsz, seq // SEQ_TILE),
        in_specs=[xspec, full(g), full(wup), full(wgate), full(cw), full(cb), full(wdown), full(gfin)],
        out_specs=xspec,
        scratch_shapes=[
            pltpu.VMEM((SEQ_TILE, d), BF16),
            pltpu.VMEM((SEQ_TILE + 2 * SUBLANES, FFN_CHUNK), F32),
            pltpu.VMEM((SEQ_TILE, FFN_CHUNK), F32),
            pltpu.VMEM((SEQ_TILE, FFN_CHUNK), BF16),
            pltpu.VMEM((SEQ_TILE, d), F32),
            pltpu.VMEM((2 * SUBLANES, f), F32),
        ],
        compiler_params=_cparams(),
        name="conv_ffn",
    )(x, g, wup, wgate, cw, cb, wdown, gfin)


def _even_kernel(x_ref, g_ref, win_ref, cw_ref, cb_ref, wa_ref, wi_ref, ba_ref, bi_ref, lam_ref,
                 sw_ref, wout_ref, o_ref,
                 h_ref, z_ref, xc_ref, xcb_ref, gate_ref, y_ref, zcarry_ref, hcarry_ref):
    hist = 3
    hrows = hist * SUBLANES
    w = xc_ref.shape[1]
    j = pl.program_id(1)

    @pl.when(j == 0)
    def _():
        zcarry_ref[...] = jnp.zeros_like(zcarry_ref)
        hcarry_ref[...] = jnp.zeros_like(hcarry_ref)

    _rmsnorm_to(x_ref, g_ref, h_ref)
    z_ref[pl.ds(hrows, SEQ_TILE), :] = jnp.dot(h_ref[...], win_ref[...], preferred_element_type=F32)
    xa_c, ga_c, hb_c, bg_c, cg_c = (pl.ds(k * w, w) for k in range(5))
    for lanes in (xa_c, hb_c, cg_c):
        _fill_history(z_ref, zcarry_ref, hist, lanes)

    def conv_body(c, carry):
        r0 = pl.multiple_of(c * ROW_CHUNK, ROW_CHUNK)
        acc = cb_ref[...] + z_ref[pl.ds(r0 + hrows, ROW_CHUNK), xa_c] * cw_ref[3:4, :]
        for k in range(1, 4):
            acc = acc + z_ref[pl.ds(r0 + hrows - k * SUBLANES, ROW_CHUNK), xa_c] * cw_ref[3 - k:4 - k, :]
        xc_ref[pl.ds(r0, ROW_CHUNK), :] = acc
        xcb_ref[pl.ds(r0, ROW_CHUNK), :] = acc.astype(BF16)
        return carry
    lax.fori_loop(0, SEQ_TILE // ROW_CHUNK, conv_body, 0)

    for blk in range(w // MXU_DIM):
        lanes = pl.ds(blk * MXU_DIM, MXU_DIM)
        gate_ref[:, pl.ds(blk * MXU_DIM, MXU_DIM)] = jnp.dot(
            xcb_ref[:, lanes], wa_ref[blk], preferred_element_type=F32)
        gate_ref[:, pl.ds(w + blk * MXU_DIM, MXU_DIM)] = jnp.dot(
            xcb_ref[:, lanes], wi_ref[blk], preferred_element_type=F32)

    lam = lam_ref[...]
    neg_c_sp = -LRU_C * (jnp.maximum(-lam, 0.0) + jnp.log1p(jnp.exp(-jnp.abs(lam))))

    def scan_body(i, carry):
        h, p = carry
        r0 = pl.multiple_of(i * SUBLANES, SUBLANES)
        rows = pl.ds(r0, SUBLANES)
        xc = xc_ref[rows, :]
        r = _sigmoid(gate_ref[rows, pl.ds(0, w)] + ba_ref[...])
        ig = _sigmoid(gate_ref[rows, pl.ds(w, w)] + bi_ref[...])
        log_a = r * neg_c_sp
        a = jnp.exp(log_a)
        mult = jnp.sqrt(1.0 - a * a)
        sub = lax.broadcasted_iota(jnp.int32, mult.shape, 0)
        t = j * SEQ_TILE + sub * SEG + i
        mult = jnp.where(t == 0, 1.0, mult)
        h = a * h + mult * ig * xc
        p = a * p
        xc_ref[rows, :] = h
        gate_ref[rows, pl.ds(0, w)] = p
        return h, p
    zeros = jnp.zeros((SUBLANES, w), F32)
    h_last, p_last = lax.fori_loop(0, SEG, scan_body, (zeros, zeros + 1.0), unroll=4)

    sub = lax.broadcasted_iota(jnp.int32, (SUBLANES, w), 0)
    h_in = jnp.where(sub == 0, hcarry_ref[...], 0.0)
    for s in range(SUBLANES - 1):
        nxt = pltpu.roll(h_last + p_last * h_in, 1, axis=0)
        h_in = jnp.where(sub == s + 1, nxt, h_in)
    hcarry_ref[...] = pltpu.roll(h_last + p_last * h_in, 1, axis=0)

    def out_body(c, carry):
        r0 = pl.multiple_of(c * ROW_CHUNK, ROW_CHUNK)
        rows = pl.ds(r0, ROW_CHUNK)
        h_rep = jnp.concatenate([h_in] * (ROW_CHUNK // SUBLANES), axis=0)
        h = xc_ref[rows, :] + gate_ref[rows, pl.ds(0, w)] * h_rep
        ya = h * jax.nn.gelu(z_ref[pl.ds(r0 + hrows, ROW_CHUNK), ga_c])
        conv = None
        for k in range(3):
            zrows = pl.ds(r0 + hrows - k * SUBLANES, ROW_CHUNK)
            term = z_ref[zrows, cg_c] * z_ref[zrows, hb_c] * sw_ref[2 - k:3 - k, :]
            conv = term if conv is None else conv + term
        yb = z_ref[pl.ds(r0 + hrows, ROW_CHUNK), bg_c] * conv
        y_ref[rows, pl.ds(0, w)] = ya.astype(BF16)
        y_ref[rows, pl.ds(w, w)] = yb.astype(BF16)
        return carry
    lax.fori_loop(0, SEQ_TILE // ROW_CHUNK, out_body, 0)

    o_ref[...] = x_ref[...] + jnp.dot(y_ref[...], wout_ref[...], preferred_element_type=F32)


def _even_call(x, g, win, cw, cb, wa, wi, ba, bi, lam, sw, wout):
    bsz, seq, d = x.shape
    zin = win.shape[1]
    w = zin // 5
    full = lambda a: pl.BlockSpec(a.shape, lambda b, j: (0,) * a.ndim)
    xspec = pl.BlockSpec((None, SEQ_TILE, d), lambda b, j: (b, j, 0))
    args = (x, g, win, cw, cb, wa, wi, ba, bi, lam, sw, wout)
    return pl.pallas_call(
        _even_kernel,
        out_shape=jax.ShapeDtypeStruct(x.shape, x.dtype),
        grid=(bsz, seq // SEQ_TILE),
        in_specs=[xspec] + [full(a) for a in args[1:]],
        out_specs=xspec,
        scratch_shapes=[
            pltpu.VMEM((SEQ_TILE, d), BF16),
            pltpu.VMEM((SEQ_TILE + 3 * SUBLANES, zin), F32),
            pltpu.VMEM((SEQ_TILE, w), F32),
            pltpu.VMEM((SEQ_TILE, w), BF16),
            pltpu.VMEM((SEQ_TILE, 2 * w), F32),
            pltpu.VMEM((SEQ_TILE, 2 * w), BF16),
            pltpu.VMEM((3 * SUBLANES, zin), F32),
            pltpu.VMEM((SUBLANES, w), F32),
        ],
        compiler_params=_cparams(),
        name="lru_sconv_mixer",
    )(*args)


def _hgrn_level_masks():
    t = np.arange(SEG)[:, None]
    s = np.arange(SEG)[None, :]
    x = t ^ s
    masks = [(x == 0)]
    for lvl in range(1, HGRN_LEVELS):
        half = 1 << (lvl - 1)
        masks.append((x >= half) & (x < 2 * half) & (t > s))
    return np.stack(masks).astype(np.float32)


def _odd_kernel(x_ref, g_ref, win_ref, pw_ref, pscale_ref, lbl_ref, ng_ref, mask_ref, wout_ref, o_ref,
                h_ref, z_ref, pa_ref, pb_ref, pcarry_ref, cum_ref, k_ref, lv_ref, aux_ref, oh_ref,
                y_ref, state_ref, *, layer):
    w = pscale_ref.shape[1]
    heads = w // LANES
    j = pl.program_id(1)
    hist = POOL_WINDOWS[-1] // 2
    hrows = hist * SUBLANES

    @pl.when(j == 0)
    def _():
        pcarry_ref[...] = jnp.zeros_like(pcarry_ref)
        state_ref[...] = jnp.zeros_like(state_ref)

    _rmsnorm_to(x_ref, g_ref, h_ref)
    z_ref[...] = jnp.dot(h_ref[...], win_ref[...], preferred_element_type=F32)
    uc_c, q_c, fz_c, v_c, gd_c = (k * w for k in range(5))

    bufs = (pa_ref, pb_ref)
    pa_ref[pl.ds(hrows, SEQ_TILE), :] = z_ref[:, pl.ds(uc_c, w)]
    for stage in range(len(POOL_WINDOWS)):
        shift = 1 << stage
        src = bufs[stage % 2]
        dst = bufs[(stage + 1) % 2]
        lanes = pl.ds(stage * LANES, w - stage * LANES)
        for gidx in range(shift):
            tail = src[pl.ds(hrows + SEQ_TILE - (shift - gidx) * SUBLANES, SUBLANES), lanes]
            rolled = pltpu.roll(tail, 1, axis=0)
            sub = lax.broadcasted_iota(jnp.int32, rolled.shape, 0)
            crow = pl.ds((shift - 1 + gidx) * SUBLANES, SUBLANES)
            prev = pcarry_ref[crow, lanes]
            src[pl.ds(hrows - (shift - gidx) * SUBLANES, SUBLANES), lanes] = jnp.where(sub == 0, prev, rolled)
            pcarry_ref[crow, lanes] = rolled

        def pool_body(c, carry, src=src, dst=dst, lanes=lanes, shift=shift):
            r0 = pl.multiple_of(c * ROW_CHUNK, ROW_CHUNK)
            dst[pl.ds(r0 + hrows, ROW_CHUNK), lanes] = (
                src[pl.ds(r0 + hrows, ROW_CHUNK), lanes]
                + src[pl.ds(r0 + hrows - shift * SUBLANES, ROW_CHUNK), lanes])
            return carry
        lax.fori_loop(0, SEQ_TILE // ROW_CHUNK, pool_body, 0)

    def pool_out_body(c, carry):
        r0 = pl.multiple_of(c * ROW_CHUNK, ROW_CHUNK)
        t = _row_time(j, c * (ROW_CHUNK // SUBLANES), (ROW_CHUNK, LANES))
        for gi, win in enumerate(POOL_WINDOWS):
            buf = bufs[(gi + 1) % 2]
            lanes = pl.ds(gi * LANES, LANES)
            cnt = jnp.minimum(t + 1, win).astype(F32)
            p = buf[pl.ds(r0 + hrows, ROW_CHUNK), lanes] / cnt - z_ref[pl.ds(r0, ROW_CHUNK), pl.ds(uc_c + gi * LANES, LANES)]
            y_ref[pl.ds(r0, ROW_CHUNK), lanes] = p.astype(BF16)
        return carry
    lax.fori_loop(0, SEQ_TILE // ROW_CHUNK, pool_out_body, 0)

    for blk in range(w // MXU_DIM):
        lanes = pl.ds(blk * MXU_DIM, MXU_DIM)
        mixed = jnp.dot(y_ref[:, lanes], pw_ref[blk], preferred_element_type=F32)
        y_ref[:, lanes] = (mixed * pscale_ref[:, lanes]).astype(BF16)

    logits = lbl_ref[...]
    sm = jnp.exp(logits - jnp.max(logits, axis=0, keepdims=True))
    sm = sm / jnp.sum(sm, axis=0, keepdims=True)
    lb = jnp.zeros((1, w), F32)
    for o in range(1, layer + 1):
        lb = lb + sm[o:o + 1, :]
    log_lb = jnp.log(lb)
    log_1m_lb = jnp.log1p(-lb)

    for hd in range(heads):
        hl = pl.ds(hd * LANES, LANES)
        la = log_lb[:, hd * LANES:(hd + 1) * LANES]
        l1 = log_1m_lb[:, hd * LANES:(hd + 1) * LANES]
        om = 1.0 - lb[:, hd * LANES:(hd + 1) * LANES]

        def gate_body(i, cum):
            rows = pl.ds(pl.multiple_of(i * SUBLANES, SUBLANES), SUBLANES)
            fz = z_ref[rows, pl.ds(fz_c + hd * LANES, LANES)]
            e = jnp.exp(-jnp.abs(fz))
            log_sig = jnp.minimum(fz, 0.0) - jnp.log1p(e)
            b = l1 + log_sig
            log_f = jnp.maximum(la, b) + jnp.log1p(jnp.exp(-jnp.abs(la - b)))
            r = 1.0 / (1.0 + e)
            k_ref[rows, :] = om * jnp.where(fz >= 0, e * r, r)
            cum = cum + log_f
            cum_ref[rows, :] = cum
            return cum
        cum_last = lax.fori_loop(0, SEG, gate_body, jnp.zeros((SUBLANES, LANES), F32), unroll=4)

        def level_body(i, carry):
            rows = pl.ds(pl.multiple_of(i * SUBLANES, SUBLANES), SUBLANES)
            q = z_ref[rows, pl.ds(q_c + hd * LANES, LANES)]
            k = k_ref[rows, :]
            cum = cum_ref[rows, :]
            lv_ref[0, 0, rows, :] = q
            lv_ref[0, 1, rows, :] = k
            for lvl in range(1, HGRN_LEVELS):
                half = 1 << (lvl - 1)
                mid = ((i >> lvl) << lvl) | (half - 1)
                ref_rows = pl.ds(pl.multiple_of(mid * SUBLANES, SUBLANES), SUBLANES)
                dec = jnp.exp(-jnp.abs(cum - cum_ref[ref_rows, :]))
                lv_ref[lvl, 0, rows, :] = q * dec
                lv_ref[lvl, 1, rows, :] = k * dec
            aux_ref[0, rows, :] = q * jnp.exp(cum)
            aux_ref[1, rows, :] = k * jnp.exp(cum_last - cum)
            aux_ref[2, rows, :] = z_ref[rows, pl.ds(v_c + hd * LANES, LANES)]
            return carry
        lax.fori_loop(0, SEG, level_body, 0, unroll=2)

        seg_decay = jnp.exp(cum_last)
        nt = (((1,), (1,)), ((), ()))
        tn = (((0,), (0,)), ((), ()))
        for s in range(SUBLANES):
            seg_rows = pl.ds(s, SEG, stride=SUBLANES)
            att = None
            for lvl in range(HGRN_LEVELS):
                qs = lv_ref[lvl, 0, seg_rows, :].astype(BF16)
                ks = lv_ref[lvl, 1, seg_rows, :].astype(BF16)
                part = lax.dot_general(qs, ks, nt, preferred_element_type=F32) * mask_ref[lvl]
                att = part if att is None else att + part
            vs = aux_ref[2, seg_rows, :].astype(BF16)
            st = state_ref[hd]
            o = jnp.dot(att.astype(BF16), vs, preferred_element_type=F32)
            o = o + lax.dot_general(aux_ref[0, seg_rows, :].astype(BF16), st.astype(BF16), nt,
                                    preferred_element_type=F32)
            upd = lax.dot_general(vs, aux_ref[1, seg_rows, :].astype(BF16), tn, preferred_element_type=F32)
            state_ref[hd] = st * seg_decay[s:s + 1, :] + upd
            oh_ref[hd, seg_rows, :] = o

    def hout_body(c, carry):
        rows = pl.ds(pl.multiple_of(c * ROW_CHUNK, ROW_CHUNK), ROW_CHUNK)
        for hd in range(heads):
            o = oh_ref[hd, rows, :]
            o = o * lax.rsqrt(jnp.mean(o * o, axis=-1, keepdims=True) + EPS) * ng_ref[:, pl.ds(hd * LANES, LANES)]
            gd = z_ref[rows, pl.ds(gd_c + hd * LANES, LANES)]
            y_ref[rows, pl.ds(w + hd * LANES, LANES)] = (o * (gd * _sigmoid(gd))).astype(BF16)
        return carry
    lax.fori_loop(0, SEQ_TILE // ROW_CHUNK, hout_body, 0)

    o_ref[...] = x_ref[...] + jnp.dot(y_ref[...], wout_ref[...], preferred_element_type=F32)


def _odd_call(x, g, win, pw, pscale, lbl, ng, masks, wout, layer):
    bsz, seq, d = x.shape
    zin = win.shape[1]
    w = zin // 5
    heads = w // LANES
    hist = POOL_WINDOWS[-1] // 2
    full = lambda a: pl.BlockSpec(a.shape, lambda b, j: (0,) * a.ndim)
    xspec = pl.BlockSpec((None, SEQ_TILE, d), lambda b, j: (b, j, 0))
    args = (x, g, win, pw, pscale, lbl, ng, masks, wout)
    return pl.pallas_call(
        functools.partial(_odd_kernel, layer=layer),
        out_shape=jax.ShapeDtypeStruct(x.shape, x.dtype),
        grid=(bsz, seq // SEQ_TILE),
        in_specs=[xspec] + [full(a) for a in args[1:]],
        out_specs=xspec,
        scratch_shapes=[
            pltpu.VMEM((SEQ_TILE, d), BF16),
            pltpu.VMEM((SEQ_TILE, zin), F32),
            pltpu.VMEM((SEQ_TILE + hist * SUBLANES, w), F32),
            pltpu.VMEM((SEQ_TILE + hist * SUBLANES, w), F32),
            pltpu.VMEM(((POOL_WINDOWS[-1] - 1) * SUBLANES, w), F32),
            pltpu.VMEM((SEQ_TILE, LANES), F32),
            pltpu.VMEM((SEQ_TILE, LANES), F32),
            pltpu.VMEM((HGRN_LEVELS, 2, SEQ_TILE, LANES), F32),
            pltpu.VMEM((3, SEQ_TILE, LANES), F32),
            pltpu.VMEM((heads, SEQ_TILE, LANES), F32),
            pltpu.VMEM((SEQ_TILE, 2 * w), BF16),
            pltpu.VMEM((heads, LANES, LANES), F32),
        ],
        compiler_params=_cparams(),
        name="pool_hgrn_mixer",
    )(*args)


def _to_segment_major(x):
    b, s, d = x.shape
    return x.reshape(b, s // SEQ_TILE, SUBLANES, SEG, d).transpose(0, 1, 3, 2, 4).reshape(b, s, d)


def _from_segment_major(x):
    b, s, d = x.shape
    return x.reshape(b, s // SEQ_TILE, SEG, SUBLANES, d).transpose(0, 1, 3, 2, 4).reshape(b, s, d)


def _block_diag_tiles(wblocks):
    nb, n, _ = wblocks.shape
    per = MXU_DIM // n
    tiles = []
    for t in range(nb // per):
        tile = jnp.zeros((MXU_DIM, MXU_DIM), wblocks.dtype)
        for k in range(per):
            tile = lax.dynamic_update_slice(tile, wblocks[t * per + k], (k * n, k * n))
        tiles.append(tile)
    return jnp.stack(tiles).astype(BF16)


def kernel(x, g_mix, g_ffn, g_final, w_in_even, w_out_even, lru_conv_w, lru_conv_b, lru_wa, lru_ba, lru_wi, lru_bi, lru_lambda, sconv_w, w_in_odd, w_out_odd, pool_w, pool_scale, hgrn_lb_logits, hgrn_norm_g, ffn_w_up, ffn_w_gate, ffn_conv_w, ffn_conv_b, ffn_w_down):
    depth = g_mix.shape[0]
    assert x.shape[1] % SEQ_TILE == 0
    row = lambda v: v.reshape(1, -1).astype(F32)
    masks = jnp.asarray(_hgrn_level_masks())
    x = _to_segment_major(x)
    for l in range(depth):
        if l % 2 == 0:
            e = l // 2
            x = _even_call(
                x, row(g_mix[l]), w_in_even[e].astype(BF16), lru_conv_w[e], row(lru_conv_b[e]),
                _block_diag_tiles(lru_wa[e]), _block_diag_tiles(lru_wi[e]), row(lru_ba[e]), row(lru_bi[e]),
                row(lru_lambda[e]), sconv_w[e], w_out_even[e].astype(BF16))
        else:
            o = l // 2
            x = _odd_call(
                x, row(g_mix[l]), w_in_odd[o].astype(BF16), _block_diag_tiles(pool_w[o]), row(pool_scale[o]),
                hgrn_lb_logits.astype(F32), row(hgrn_norm_g[o]), masks, w_out_odd[o].astype(BF16), o)
        x = _ffn_call(
            x, row(g_ffn[l]), ffn_w_up[l].astype(BF16), ffn_w_gate[l].astype(BF16), ffn_conv_w[l],
            row(ffn_conv_b[l]), ffn_w_down[l].astype(BF16), row(g_final), l == depth - 1)
    return _from_segment_major(x)
```

```python
import functools

import jax
import jax.numpy as jnp
import numpy as np
from jax import lax
from jax.experimental import pallas as pl
from jax.experimental.pallas import tpu as pltpu

F32 = jnp.float32
BF16 = jnp.bfloat16

SUBLANES = 8
LANES = 128
MXU_DIM = 256
VMEM_LIMIT_BYTES = 56 * 1024 * 1024

EPS = 1e-6
LRU_C = 8.0
POOL_WINDOWS = (2, 4, 8, 16)

SEQ_TILE = 512
SEG = SEQ_TILE // SUBLANES
ROW_CHUNK = 64
FFN_CHUNK = 256
TIME_BITS = SEQ_TILE.bit_length() - 1


def _cparams():
    return pltpu.CompilerParams(
        dimension_semantics=("arbitrary", "arbitrary"),
        vmem_limit_bytes=VMEM_LIMIT_BYTES,
    )


def _sigmoid(v):
    return 0.5 * jnp.tanh(0.5 * v) + 0.5


def _rmsnorm_rows(x, g):
    ms = jnp.mean(x * x, axis=-1, keepdims=True)
    return x * lax.rsqrt(ms + EPS) * g


def _rmsnorm_to(x_ref, g_ref, h_ref):
    def body(c, carry):
        r0 = pl.multiple_of(c * ROW_CHUNK, ROW_CHUNK)
        x = x_ref[pl.ds(r0, ROW_CHUNK), :]
        h_ref[pl.ds(r0, ROW_CHUNK), :] = _rmsnorm_rows(x, g_ref[...]).astype(h_ref.dtype)
        return carry
    lax.fori_loop(0, SEQ_TILE // ROW_CHUNK, body, 0)


def _fill_history(buf_ref, carry_ref, groups, lanes):
    for g in range(groups):
        tail = buf_ref[pl.ds(SEQ_TILE + g * SUBLANES, SUBLANES), lanes]
        rolled = pltpu.roll(tail, 1, axis=0)
        sub = lax.broadcasted_iota(jnp.int32, rolled.shape, 0)
        prev = carry_ref[pl.ds(g * SUBLANES, SUBLANES), lanes]
        buf_ref[pl.ds(g * SUBLANES, SUBLANES), lanes] = jnp.where(sub == 0, prev, rolled)
        carry_ref[pl.ds(g * SUBLANES, SUBLANES), lanes] = rolled


def _row_time(j, i0, shape):
    row = lax.broadcasted_iota(jnp.int32, shape, 0)
    sub = row & (SUBLANES - 1)
    grp = row >> (SUBLANES.bit_length() - 1)
    return j * SEQ_TILE + sub * SEG + grp + i0


def _ffn_kernel(x_ref, g_ref, wup_ref, wgate_ref, cw_ref, cb_ref, wdown_ref, gfin_ref, o_ref,
                h_ref, up_ref, gate_ref, act_ref, carry_ref, *, final_norm):
    hist = 2
    hrows = hist * SUBLANES
    ffn_dim = wup_ref.shape[1]

    @pl.when(pl.program_id(1) == 0)
    def _():
        carry_ref[...] = jnp.zeros_like(carry_ref)

    for r in range(SEQ_TILE // ROW_CHUNK):
        rows = pl.ds(r * ROW_CHUNK, ROW_CHUNK)
        h_ref[rows, :] = _rmsnorm_rows(x_ref[rows, :], g_ref[...]).astype(BF16)

    for c in range(ffn_dim // FFN_CHUNK):
        cols = pl.ds(c * FFN_CHUNK, FFN_CHUNK)
        up = up_ref.at[c % 2]
        gate = gate_ref.at[c % 2]
        up[pl.ds(hrows, SEQ_TILE), :] = jnp.dot(h_ref[...], wup_ref[:, cols], preferred_element_type=F32)
        gate[...] = jnp.dot(h_ref[...], wgate_ref[:, cols], preferred_element_type=F32)
        for g in range(hist):
            tail = up[pl.ds(SEQ_TILE + g * SUBLANES, SUBLANES), :]
            rolled = pltpu.roll(tail, 1, axis=0)
            sub = lax.broadcasted_iota(jnp.int32, rolled.shape, 0)
            prev = carry_ref[pl.ds(g * SUBLANES, SUBLANES), cols]
            up[pl.ds(g * SUBLANES, SUBLANES), :] = jnp.where(sub == 0, prev, rolled)
            carry_ref[pl.ds(g * SUBLANES, SUBLANES), cols] = rolled
        w0 = cw_ref[0:1, cols]
        w1 = cw_ref[1:2, cols]
        w2 = cw_ref[2:3, cols]
        b = cb_ref[:, cols]
        for r in range(SEQ_TILE // ROW_CHUNK):
            r0 = r * ROW_CHUNK
            u = (up[pl.ds(r0 + hrows, ROW_CHUNK), :] * w2
                 + up[pl.ds(r0 + hrows - SUBLANES, ROW_CHUNK), :] * w1
                 + up[pl.ds(r0, ROW_CHUNK), :] * w0 + b)
            a = jax.nn.gelu(u) * gate[pl.ds(r0, ROW_CHUNK), :]
            act_ref[pl.ds(r0, ROW_CHUNK), cols] = a.astype(BF16)

    o_ref[...] = x_ref[...] + jnp.dot(act_ref[...], wdown_ref[...], preferred_element_type=F32)
    if final_norm:
        for r in range(SEQ_TILE // ROW_CHUNK):
            rows = pl.ds(r * ROW_CHUNK, ROW_CHUNK)
            o_ref[rows, :] = _rmsnorm_rows(o_ref[rows, :], gfin_ref[...])


def _ffn_call(x, g, wup, wgate, cw, cb, wdown, gfin, final_norm):
    bsz, seq, d = x.shape
    f = wup.shape[1]
    full = lambda a: pl.BlockSpec(a.shape, lambda b, j: (0,) * a.ndim)
    xspec = pl.BlockSpec((None, SEQ_TILE, d), lambda b, j: (b, j, 0))
    return pl.pallas_call(
        functools.partial(_ffn_kernel, final_norm=final_norm),
        out_shape=jax.ShapeDtypeStruct(x.shape, x.dtype),
        grid=(bsz, seq // SEQ_TILE),
        in_specs=[xspec, full(g), full(wup), full(wgate), full(cw), full(cb), full(wdown), full(gfin)],
        out_specs=xspec,
        scratch_shapes=[
            pltpu.VMEM((SEQ_TILE, d), BF16),
            pltpu.VMEM((2, SEQ_TILE + 2 * SUBLANES, FFN_CHUNK), F32),
            pltpu.VMEM((2, SEQ_TILE, FFN_CHUNK), F32),
            pltpu.VMEM((SEQ_TILE, f), BF16),
            pltpu.VMEM((2 * SUBLANES, f), F32),
        ],
        compiler_params=_cparams(),
        name="conv_ffn",
    )(x, g, wup, wgate, cw, cb, wdown, gfin)


def _even_kernel(x_ref, g_ref, win_ref, cw_ref, cb_ref, wa_ref, wi_ref, ba_ref, bi_ref, lam_ref,
                 sw_ref, wout_ref, o_ref,
                 h_ref, z_ref, xc_ref, xcb_ref, gate_ref, y_ref, zcarry_ref, hcarry_ref):
    hist = 3
    hrows = hist * SUBLANES
    w = xc_ref.shape[1]
    j = pl.program_id(1)

    @pl.when(j == 0)
    def _():
        zcarry_ref[...] = jnp.zeros_like(zcarry_ref)
        hcarry_ref[...] = jnp.zeros_like(hcarry_ref)

    _rmsnorm_to(x_ref, g_ref, h_ref)
    z_ref[pl.ds(hrows, SEQ_TILE), :] = jnp.dot(h_ref[...], win_ref[...], preferred_element_type=F32)
    xa_c, ga_c, hb_c, bg_c, cg_c = (pl.ds(k * w, w) for k in range(5))
    for lanes in (xa_c, hb_c, cg_c):
        _fill_history(z_ref, zcarry_ref, hist, lanes)

    def conv_body(c, carry):
        r0 = pl.multiple_of(c * ROW_CHUNK, ROW_CHUNK)
        acc = cb_ref[...] + z_ref[pl.ds(r0 + hrows, ROW_CHUNK), xa_c] * cw_ref[3:4, :]
        for k in range(1, 4):
            acc = acc + z_ref[pl.ds(r0 + hrows - k * SUBLANES, ROW_CHUNK), xa_c] * cw_ref[3 - k:4 - k, :]
        xc_ref[pl.ds(r0, ROW_CHUNK), :] = acc
        xcb_ref[pl.ds(r0, ROW_CHUNK), :] = acc.astype(BF16)
        return carry
    lax.fori_loop(0, SEQ_TILE // ROW_CHUNK, conv_body, 0)

    for blk in range(w // MXU_DIM):
        lanes = pl.ds(blk * MXU_DIM, MXU_DIM)
        gate_ref[:, pl.ds(blk * MXU_DIM, MXU_DIM)] = jnp.dot(
            xcb_ref[:, lanes], wa_ref[blk], preferred_element_type=F32)
        gate_ref[:, pl.ds(w + blk * MXU_DIM, MXU_DIM)] = jnp.dot(
            xcb_ref[:, lanes], wi_ref[blk], preferred_element_type=F32)

    lam = lam_ref[...]
    neg_c_sp = -LRU_C * (jnp.maximum(-lam, 0.0) + jnp.log1p(jnp.exp(-jnp.abs(lam))))

    def gate_body(c, carry):
        rows = pl.ds(pl.multiple_of(c * ROW_CHUNK, ROW_CHUNK), ROW_CHUNK)
        r = _sigmoid(gate_ref[rows, pl.ds(0, w)] + ba_ref[...])
        ig = _sigmoid(gate_ref[rows, pl.ds(w, w)] + bi_ref[...])
        a = jnp.exp(r * neg_c_sp)
        mult = jnp.sqrt(1.0 - a * a)
        t = _row_time(j, c * (ROW_CHUNK // SUBLANES), mult.shape)
        mult = jnp.where(t == 0, 1.0, mult)
        xc_ref[rows, :] = mult * ig * xc_ref[rows, :]
        gate_ref[rows, pl.ds(0, w)] = a
        return carry
    lax.fori_loop(0, SEQ_TILE // ROW_CHUNK, gate_body, 0)

    def scan_body(i, carry):
        h, p = carry
        rows = pl.ds(pl.multiple_of(i * SUBLANES, SUBLANES), SUBLANES)
        a = gate_ref[rows, pl.ds(0, w)]
        h = a * h + xc_ref[rows, :]
        p = a * p
        xc_ref[rows, :] = h
        gate_ref[rows, pl.ds(0, w)] = p
        return h, p
    zeros = jnp.zeros((SUBLANES, w), F32)
    h_last, p_last = lax.fori_loop(0, SEG, scan_body, (zeros, zeros + 1.0), unroll=8)

    sub = lax.broadcasted_iota(jnp.int32, (SUBLANES, w), 0)
    h_in = jnp.where(sub == 0, hcarry_ref[...], 0.0)
    for s in range(SUBLANES - 1):
        nxt = pltpu.roll(h_last + p_last * h_in, 1, axis=0)
        h_in = jnp.where(sub == s + 1, nxt, h_in)
    hcarry_ref[...] = pltpu.roll(h_last + p_last * h_in, 1, axis=0)

    def out_body(c, carry):
        r0 = pl.multiple_of(c * ROW_CHUNK, ROW_CHUNK)
        rows = pl.ds(r0, ROW_CHUNK)
        h_rep = jnp.concatenate([h_in] * (ROW_CHUNK // SUBLANES), axis=0)
        h = xc_ref[rows, :] + gate_ref[rows, pl.ds(0, w)] * h_rep
        ya = h * jax.nn.gelu(z_ref[pl.ds(r0 + hrows, ROW_CHUNK), ga_c])
        conv = None
        for k in range(3):
            zrows = pl.ds(r0 + hrows - k * SUBLANES, ROW_CHUNK)
            term = z_ref[zrows, cg_c] * z_ref[zrows, hb_c] * sw_ref[2 - k:3 - k, :]
            conv = term if conv is None else conv + term
        yb = z_ref[pl.ds(r0 + hrows, ROW_CHUNK), bg_c] * conv
        y_ref[rows, pl.ds(0, w)] = ya.astype(BF16)
        y_ref[rows, pl.ds(w, w)] = yb.astype(BF16)
        return carry
    lax.fori_loop(0, SEQ_TILE // ROW_CHUNK, out_body, 0)

    o_ref[...] = x_ref[...] + jnp.dot(y_ref[...], wout_ref[...], preferred_element_type=F32)


def _even_call(x, g, win, cw, cb, wa, wi, ba, bi, lam, sw, wout):
    bsz, seq, d = x.shape
    zin = win.shape[1]
    w = zin // 5
    full = lambda a: pl.BlockSpec(a.shape, lambda b, j: (0,) * a.ndim)
    xspec = pl.BlockSpec((None, SEQ_TILE, d), lambda b, j: (b, j, 0))
    args = (x, g, win, cw, cb, wa, wi, ba, bi, lam, sw, wout)
    return pl.pallas_call(
        _even_kernel,
        out_shape=jax.ShapeDtypeStruct(x.shape, x.dtype),
        grid=(bsz, seq // SEQ_TILE),
        in_specs=[xspec] + [full(a) for a in args[1:]],
        out_specs=xspec,
        scratch_shapes=[
            pltpu.VMEM((SEQ_TILE, d), BF16),
            pltpu.VMEM((SEQ_TILE + 3 * SUBLANES, zin), F32),
            pltpu.VMEM((SEQ_TILE, w), F32),
            pltpu.VMEM((SEQ_TILE, w), BF16),
            pltpu.VMEM((SEQ_TILE, 2 * w), F32),
            pltpu.VMEM((SEQ_TILE, 2 * w), BF16),
            pltpu.VMEM((3 * SUBLANES, zin), F32),
            pltpu.VMEM((SUBLANES, w), F32),
        ],
        compiler_params=_cparams(),
        name="lru_sconv_mixer",
    )(*args)


def _pair_levels():
    p = np.arange(SEQ_TILE)
    t = (p % SUBLANES) * SEG + p // SUBLANES
    x = t[:, None] ^ t[None, :]
    lvl = np.full(x.shape, -1.0, np.float32)
    for b in range(TIME_BITS):
        lvl[(x >> b) == 1] = b
    return lvl


def _sublane_row(v, s):
    return jnp.broadcast_to(v[s:s + 1, :], v.shape)


def _odd_kernel(x_ref, g_ref, win_ref, pw_ref, pscale_ref, lbl_ref, ng_ref, lvl_ref, wout_ref, o_ref,
                h_ref, z_ref, pa_ref, pb_ref, pcarry_ref, cum_ref, k_ref, lvq_ref, lvk_ref, qh_ref, kh_ref,
                vb_ref, att_ref, oh_ref, y_ref, state_ref, *, layer):
    w = pscale_ref.shape[1]
    heads = w // LANES
    j = pl.program_id(1)
    hist = POOL_WINDOWS[-1] // 2
    hrows = hist * SUBLANES

    @pl.when(j == 0)
    def _():
        pcarry_ref[...] = jnp.zeros_like(pcarry_ref)
        state_ref[...] = jnp.zeros_like(state_ref)

    _rmsnorm_to(x_ref, g_ref, h_ref)
    z_ref[...] = jnp.dot(h_ref[...], win_ref[...], preferred_element_type=F32)
    uc_c, q_c, fz_c, v_c, gd_c = (k * w for k in range(5))

    bufs = (pa_ref, pb_ref)
    pa_ref[pl.ds(hrows, SEQ_TILE), :] = z_ref[:, pl.ds(uc_c, w)]
    for stage in range(len(POOL_WINDOWS)):
        shift = 1 << stage
        src = bufs[stage % 2]
        dst = bufs[(stage + 1) % 2]
        lanes = pl.ds(stage * LANES, w - stage * LANES)
        for gidx in range(shift):
            tail = src[pl.ds(hrows + SEQ_TILE - (shift - gidx) * SUBLANES, SUBLANES), lanes]
            rolled = pltpu.roll(tail, 1, axis=0)
            sub = lax.broadcasted_iota(jnp.int32, rolled.shape, 0)
            crow = pl.ds((shift - 1 + gidx) * SUBLANES, SUBLANES)
            prev = pcarry_ref[crow, lanes]
            src[pl.ds(hrows - (shift - gidx) * SUBLANES, SUBLANES), lanes] = jnp.where(sub == 0, prev, rolled)
            pcarry_ref[crow, lanes] = rolled

        def pool_body(c, carry, src=src, dst=dst, lanes=lanes, shift=shift):
            r0 = pl.multiple_of(c * ROW_CHUNK, ROW_CHUNK)
            dst[pl.ds(r0 + hrows, ROW_CHUNK), lanes] = (
                src[pl.ds(r0 + hrows, ROW_CHUNK), lanes]
                + src[pl.ds(r0 + hrows - shift * SUBLANES, ROW_CHUNK), lanes])
            return carry
        lax.fori_loop(0, SEQ_TILE // ROW_CHUNK, pool_body, 0)

    def pool_out_body(c, carry):
        r0 = pl.multiple_of(c * ROW_CHUNK, ROW_CHUNK)
        t = _row_time(j, c * (ROW_CHUNK // SUBLANES), (ROW_CHUNK, LANES))
        for gi, win in enumerate(POOL_WINDOWS):
            buf = bufs[(gi + 1) % 2]
            lanes = pl.ds(gi * LANES, LANES)
            cnt = jnp.minimum(t + 1, win).astype(F32)
            p = buf[pl.ds(r0 + hrows, ROW_CHUNK), lanes] / cnt - z_ref[pl.ds(r0, ROW_CHUNK), pl.ds(uc_c + gi * LANES, LANES)]
            y_ref[pl.ds(r0, ROW_CHUNK), lanes] = p.astype(BF16)
        return carry
    lax.fori_loop(0, SEQ_TILE // ROW_CHUNK, pool_out_body, 0)

    for blk in range(w // MXU_DIM):
        lanes = pl.ds(blk * MXU_DIM, MXU_DIM)
        mixed = jnp.dot(y_ref[:, lanes], pw_ref[blk], preferred_element_type=F32)
        y_ref[:, lanes] = (mixed * pscale_ref[:, lanes]).astype(BF16)

    logits = lbl_ref[...]
    sm = jnp.exp(logits - jnp.max(logits, axis=0, keepdims=True))
    sm = sm / jnp.sum(sm, axis=0, keepdims=True)
    lb = jnp.zeros((1, w), F32)
    for o in range(1, layer + 1):
        lb = lb + sm[o:o + 1, :]
    log_lb = jnp.log(lb)
    log_1m_lb = jnp.log1p(-lb)

    seg_bits = SEG.bit_length() - 1
    pair_rows = 2 * SUBLANES
    sub8 = lax.broadcasted_iota(jnp.int32, (SUBLANES, LANES), 0)
    row16 = lax.broadcasted_iota(jnp.int32, (pair_rows, LANES), 0)
    sub16 = row16 & (SUBLANES - 1)
    nt = (((1,), (1,)), ((), ()))
    tn = (((0,), (0,)), ((), ()))
    twice = lambda v: jnp.concatenate([v, v], axis=0)

    for hd in range(heads):
        la = log_lb[:, hd * LANES:(hd + 1) * LANES]
        l1 = log_1m_lb[:, hd * LANES:(hd + 1) * LANES]
        om = 1.0 - lb[:, hd * LANES:(hd + 1) * LANES]
        q_l = pl.ds(q_c + hd * LANES, LANES)
        fz_l = pl.ds(fz_c + hd * LANES, LANES)
        v_l = pl.ds(v_c + hd * LANES, LANES)

        def gate_body(c, cum, fz_l=fz_l, la=la, l1=l1, om=om):
            rows = pl.ds(pl.multiple_of(c * ROW_CHUNK, ROW_CHUNK), ROW_CHUNK)
            fz = z_ref[rows, fz_l]
            e = jnp.exp(-jnp.abs(fz))
            log_sig = jnp.minimum(fz, 0.0) - jnp.log1p(e)
            b = l1 + log_sig
            log_f = jnp.maximum(la, b) + jnp.log1p(jnp.exp(-jnp.abs(la - b)))
            r = 1.0 / (1.0 + e)
            k_ref[rows, :] = om * jnp.where(fz >= 0, e * r, r)
            parts = []
            for g in range(ROW_CHUNK // SUBLANES):
                cum = cum + log_f[g * SUBLANES:(g + 1) * SUBLANES, :]
                parts.append(cum)
            cum_ref[rows, :] = jnp.concatenate(parts, axis=0)
            return cum
        cum_last = lax.fori_loop(0, SEQ_TILE // ROW_CHUNK, gate_body, jnp.zeros((SUBLANES, LANES), F32))

        incl = cum_last
        for sh in (1, 2, 4):
            incl = incl + jnp.where(sub8 >= sh, pltpu.roll(incl, sh, axis=0), 0.0)
        offset = incl - cum_last
        total = _sublane_row(incl, SUBLANES - 1)

        seg_levels = []
        for c in range(TIME_BITS - seg_bits):
            upper8 = ((sub8 >> c) & 1) == 1
            incl_mid = jnp.zeros_like(incl)
            for s_mid in sorted({((s >> (c + 1)) << (c + 1)) | ((1 << c) - 1) for s in range(SUBLANES)}):
                in_block = (sub8 >> (c + 1)) == (s_mid >> (c + 1))
                incl_mid = jnp.where(in_block, _sublane_row(incl, s_mid), incl_mid)
            const = jnp.where(upper8, offset - incl_mid, incl_mid - incl + cum_last)
            seg_levels.append((((sub16 >> c) & 1) == 1, twice(const)))
        q_const = twice(offset)
        k_const = twice(cum_last + total - incl)

        def factor_body(m, carry, q_l=q_l, seg_levels=seg_levels, q_const=q_const, k_const=k_const):
            rows = pl.ds(pl.multiple_of(m * pair_rows, pair_rows), pair_rows)
            q = z_ref[rows, q_l]
            k = k_ref[rows, :]
            cum = cum_ref[rows, :]
            ncum = -cum
            for b in range(TIME_BITS):
                if b < seg_bits:
                    if b == 0:
                        mid = 2 * m
                        upper = row16 >= SUBLANES
                    else:
                        mid = ((m >> b) << (b + 1)) | ((1 << b) - 1)
                        upper = ((m >> (b - 1)) & 1) == 1
                    ref = cum_ref[pl.ds(pl.multiple_of(mid * SUBLANES, SUBLANES), SUBLANES), :]
                    dec = jnp.exp(-jnp.abs(cum - twice(ref)))
                else:
                    upper, const = seg_levels[b - seg_bits]
                    dec = jnp.exp(jnp.where(upper, cum, ncum) + const)
                wgt = jnp.where(upper, q, k) * dec
                lvq_ref[b, rows, :] = jnp.where(upper, wgt, 0.0).astype(BF16)
                lvk_ref[b, rows, :] = jnp.where(upper, 0.0, wgt).astype(BF16)
            qh_ref[rows, :] = (q * jnp.exp(cum + q_const)).astype(BF16)
            kh_ref[rows, :] = (k * jnp.exp(ncum + k_const)).astype(BF16)
            return carry
        lax.fori_loop(0, SEQ_TILE // pair_rows, factor_body, 0, unroll=2)

        vb_ref[...] = z_ref[:, v_l].astype(BF16)
        for b in range(TIME_BITS):
            part = lax.dot_general(lvq_ref[b], lvk_ref[b], nt, preferred_element_type=F32)
            hit = lvl_ref[...] == float(b)
            att_ref[...] = jnp.where(hit, part, 0.0 if b == 0 else att_ref[...])

        st = state_ref[hd]
        diag = jnp.sum(z_ref[:, q_l] * k_ref[...], axis=-1, keepdims=True)
        o = jnp.dot(att_ref[...].astype(BF16), vb_ref[...], preferred_element_type=F32)
        o = o + lax.dot_general(qh_ref[...], st.astype(BF16), nt, preferred_element_type=F32)
        oh_ref[hd] = o + diag * z_ref[:, v_l]
        upd = lax.dot_general(vb_ref[...], kh_ref[...], tn, preferred_element_type=F32)
        state_ref[hd] = st * jnp.exp(total[0:1, :]) + upd

    def hout_body(c, carry):
        rows = pl.ds(pl.multiple_of(c * ROW_CHUNK, ROW_CHUNK), ROW_CHUNK)
        for hd in range(heads):
            o = oh_ref[hd, rows, :]
            o = o * lax.rsqrt(jnp.mean(o * o, axis=-1, keepdims=True) + EPS) * ng_ref[:, pl.ds(hd * LANES, LANES)]
            gd = z_ref[rows, pl.ds(gd_c + hd * LANES, LANES)]
            y_ref[rows, pl.ds(w + hd * LANES, LANES)] = (o * (gd * _sigmoid(gd))).astype(BF16)
        return carry
    lax.fori_loop(0, SEQ_TILE // ROW_CHUNK, hout_body, 0)

    o_ref[...] = x_ref[...] + jnp.dot(y_ref[...], wout_ref[...], preferred_element_type=F32)


def _odd_call(x, g, win, pw, pscale, lbl, ng, levels, wout, layer):
    bsz, seq, d = x.shape
    zin = win.shape[1]
    w = zin // 5
    heads = w // LANES
    hist = POOL_WINDOWS[-1] // 2
    full = lambda a: pl.BlockSpec(a.shape, lambda b, j: (0,) * a.ndim)
    xspec = pl.BlockSpec((None, SEQ_TILE, d), lambda b, j: (b, j, 0))
    args = (x, g, win, pw, pscale, lbl, ng, levels, wout)
    return pl.pallas_call(
        functools.partial(_odd_kernel, layer=layer),
        out_shape=jax.ShapeDtypeStruct(x.shape, x.dtype),
        grid=(bsz, seq // SEQ_TILE),
        in_specs=[xspec] + [full(a) for a in args[1:]],
        out_specs=xspec,
        scratch_shapes=[
            pltpu.VMEM((SEQ_TILE, d), BF16),
            pltpu.VMEM((SEQ_TILE, zin), F32),
            pltpu.VMEM((SEQ_TILE + hist * SUBLANES, w), F32),
            pltpu.VMEM((SEQ_TILE + hist * SUBLANES, w), F32),
            pltpu.VMEM(((POOL_WINDOWS[-1] - 1) * SUBLANES, w), F32),
            pltpu.VMEM((SEQ_TILE, LANES), F32),
            pltpu.VMEM((SEQ_TILE, LANES), F32),
            pltpu.VMEM((TIME_BITS, SEQ_TILE, LANES), BF16),
            pltpu.VMEM((TIME_BITS, SEQ_TILE, LANES), BF16),
            pltpu.VMEM((SEQ_TILE, LANES), BF16),
            pltpu.VMEM((SEQ_TILE, LANES), BF16),
            pltpu.VMEM((SEQ_TILE, LANES), BF16),
            pltpu.VMEM((SEQ_TILE, SEQ_TILE), F32),
            pltpu.VMEM((heads, SEQ_TILE, LANES), F32),
            pltpu.VMEM((SEQ_TILE, 2 * w), BF16),
            pltpu.VMEM((heads, LANES, LANES), F32),
        ],
        compiler_params=_cparams(),
        name="pool_hgrn_mixer",
    )(*args)


def _to_segment_major(x):
    b, s, d = x.shape
    return x.reshape(b, s // SEQ_TILE, SUBLANES, SEG, d).transpose(0, 1, 3, 2, 4).reshape(b, s, d)


def _from_segment_major(x):
    b, s, d = x.shape
    return x.reshape(b, s // SEQ_TILE, SEG, SUBLANES, d).transpose(0, 1, 3, 2, 4).reshape(b, s, d)


def _block_diag_tiles(wblocks):
    nb, n, _ = wblocks.shape
    per = MXU_DIM // n
    tiles = []
    for t in range(nb // per):
        tile = jnp.zeros((MXU_DIM, MXU_DIM), wblocks.dtype)
        for k in range(per):
            tile = lax.dynamic_update_slice(tile, wblocks[t * per + k], (k * n, k * n))
        tiles.append(tile)
    return jnp.stack(tiles).astype(BF16)


def kernel(x, g_mix, g_ffn, g_final, w_in_even, w_out_even, lru_conv_w, lru_conv_b, lru_wa, lru_ba, lru_wi, lru_bi, lru_lambda, sconv_w, w_in_odd, w_out_odd, pool_w, pool_scale, hgrn_lb_logits, hgrn_norm_g, ffn_w_up, ffn_w_gate, ffn_conv_w, ffn_conv_b, ffn_w_down):
    depth = g_mix.shape[0]
    assert x.shape[1] % SEQ_TILE == 0
    row = lambda v: v.reshape(1, -1).astype(F32)
    levels = jnp.asarray(_pair_levels())
    x = _to_segment_major(x)
    for l in range(depth):
        if l % 2 == 0:
            e = l // 2
            x = _even_call(
                x, row(g_mix[l]), w_in_even[e].astype(BF16), lru_conv_w[e], row(lru_conv_b[e]),
                _block_diag_tiles(lru_wa[e]), _block_diag_tiles(lru_wi[e]), row(lru_ba[e]), row(lru_bi[e]),
                row(lru_lambda[e]), sconv_w[e], w_out_even[e].astype(BF16))
        else:
            o = l // 2
            x = _odd_call(
                x, row(g_mix[l]), w_in_odd[o].astype(BF16), _block_diag_tiles(pool_w[o]), row(pool_scale[o]),
                hgrn_lb_logits.astype(F32), row(hgrn_norm_g[o]), levels, w_out_odd[o].astype(BF16), o)
        x = _ffn_call(
            x, row(g_ffn[l]), ffn_w_up[l].astype(BF16), ffn_w_gate[l].astype(BF16), ffn_conv_w[l],
            row(ffn_conv_b[l]), ffn_w_down[l].astype(BF16), row(g_final), l == depth - 1)
    return _from_segment_major(x)
```

```python
import functools

import jax
import jax.numpy as jnp
import numpy as np
from jax import lax
from jax.experimental import pallas as pl
from jax.experimental.pallas import tpu as pltpu

F32 = jnp.float32
BF16 = jnp.bfloat16

SUBLANES = 8
LANES = 128
MXU_DIM = 256
VMEM_LIMIT_BYTES = 56 * 1024 * 1024

EPS = 1e-6
LRU_C = 8.0
POOL_WINDOWS = (2, 4, 8, 16)

SEQ_TILE = 512
PT = 256
NSUB = SEQ_TILE // PT
SEG = PT // SUBLANES
SEG_BITS = SEG.bit_length() - 1
TIME_BITS = PT.bit_length() - 1
ROW_CHUNK = 64
CHUNKS_PER_BLOCK = PT // ROW_CHUNK
FFN_CHUNK = 256
PAIR_ROWS = 2 * SUBLANES


def _cparams():
    return pltpu.CompilerParams(
        dimension_semantics=("arbitrary", "arbitrary"),
        vmem_limit_bytes=VMEM_LIMIT_BYTES,
    )


def _sigmoid(v):
    return 0.5 * jnp.tanh(0.5 * v) + 0.5


def _rmsnorm_rows(x, g):
    ms = jnp.mean(x * x, axis=-1, keepdims=True)
    return x * lax.rsqrt(ms + EPS) * g


def _rmsnorm_to(x_ref, g_ref, h_ref):
    for r in range(SEQ_TILE // ROW_CHUNK):
        rows = pl.ds(r * ROW_CHUNK, ROW_CHUNK)
        h_ref[rows, :] = _rmsnorm_rows(x_ref[rows, :], g_ref[...]).astype(h_ref.dtype)


def _fill_history(buf_ref, base, carry_ref, groups, lanes, carry_lanes=None):
    carry_lanes = lanes if carry_lanes is None else carry_lanes
    for g in range(groups):
        tail = buf_ref[pl.ds(base + PT + g * SUBLANES, SUBLANES), lanes]
        rolled = pltpu.roll(tail, 1, axis=0)
        sub = lax.broadcasted_iota(jnp.int32, rolled.shape, 0)
        prev = carry_ref[pl.ds(g * SUBLANES, SUBLANES), carry_lanes]
        buf_ref[pl.ds(base + g * SUBLANES, SUBLANES), lanes] = jnp.where(sub == 0, prev, rolled)
        carry_ref[pl.ds(g * SUBLANES, SUBLANES), carry_lanes] = rolled


def _row_time(t0, shape):
    row = lax.broadcasted_iota(jnp.int32, shape, 0)
    sub = row & (SUBLANES - 1)
    grp = row >> (SUBLANES.bit_length() - 1)
    return t0 + sub * SEG + grp


def _ffn_kernel(x_ref, g_ref, wup_ref, wgate_ref, cw_ref, cb_ref, wdown_ref, gfin_ref, o_ref,
                h_ref, up_ref, gate_ref, act_ref, carry_ref, *, final_norm):
    hist = 2
    hrows = hist * SUBLANES
    stride = hrows + PT
    ffn_dim = wup_ref.shape[1]

    @pl.when(pl.program_id(1) == 0)
    def _():
        carry_ref[...] = jnp.zeros_like(carry_ref)

    _rmsnorm_to(x_ref, g_ref, h_ref)

    for c in range(ffn_dim // FFN_CHUNK):
        cols = pl.ds(c * FFN_CHUNK, FFN_CHUNK)
        up = up_ref.at[c % 2]
        gate = gate_ref.at[c % 2]
        proj = jnp.dot(h_ref[...], wup_ref[:, cols], preferred_element_type=F32)
        for st in range(NSUB):
            up[pl.ds(st * stride + hrows, PT), :] = proj[st * PT:(st + 1) * PT, :]
        gate[...] = jnp.dot(h_ref[...], wgate_ref[:, cols], preferred_element_type=F32)
        w0 = cw_ref[0:1, cols]
        w1 = cw_ref[1:2, cols]
        w2 = cw_ref[2:3, cols]
        b = cb_ref[:, cols]
        for st in range(NSUB):
            _fill_history(up, st * stride, carry_ref, hist, slice(None), cols)
            for r in range(CHUNKS_PER_BLOCK):
                r0 = st * stride + r * ROW_CHUNK
                u = (up[pl.ds(r0 + hrows, ROW_CHUNK), :] * w2
                     + up[pl.ds(r0 + hrows - SUBLANES, ROW_CHUNK), :] * w1
                     + up[pl.ds(r0, ROW_CHUNK), :] * w0 + b)
                rows = pl.ds(st * PT + r * ROW_CHUNK, ROW_CHUNK)
                act_ref[rows, cols] = (jax.nn.gelu(u) * gate[rows, :]).astype(BF16)

    o_ref[...] = x_ref[...] + jnp.dot(act_ref[...], wdown_ref[...], preferred_element_type=F32)
    if final_norm:
        for r in range(SEQ_TILE // ROW_CHUNK):
            rows = pl.ds(r * ROW_CHUNK, ROW_CHUNK)
            o_ref[rows, :] = _rmsnorm_rows(o_ref[rows, :], gfin_ref[...])


def _ffn_call(x, g, wup, wgate, cw, cb, wdown, gfin, final_norm):
    bsz, seq, d = x.shape
    f = wup.shape[1]
    full = lambda a: pl.BlockSpec(a.shape, lambda b, j: (0,) * a.ndim)
    xspec = pl.BlockSpec((None, SEQ_TILE, d), lambda b, j: (b, j, 0))
    return pl.pallas_call(
        functools.partial(_ffn_kernel, final_norm=final_norm),
        out_shape=jax.ShapeDtypeStruct(x.shape, x.dtype),
        grid=(bsz, seq // SEQ_TILE),
        in_specs=[xspec, full(g), full(wup), full(wgate), full(cw), full(cb), full(wdown), full(gfin)],
        out_specs=xspec,
        scratch_shapes=[
            pltpu.VMEM((SEQ_TILE, d), BF16),
            pltpu.VMEM((2, NSUB * (PT + 2 * SUBLANES), FFN_CHUNK), F32),
            pltpu.VMEM((2, SEQ_TILE, FFN_CHUNK), F32),
            pltpu.VMEM((SEQ_TILE, f), BF16),
            pltpu.VMEM((2 * SUBLANES, f), F32),
        ],
        compiler_params=_cparams(),
        name="conv_ffn",
    )(x, g, wup, wgate, cw, cb, wdown, gfin)


def _even_kernel(x_ref, g_ref, win_ref, cw_ref, cb_ref, wa_ref, wi_ref, ba_ref, bi_ref, lam_ref,
                 sw_ref, wout_ref, o_ref,
                 h_ref, z_ref, xc_ref, xcb_ref, gate_ref, y_ref, zcarry_ref, hcarry_ref):
    hist = 3
    hrows = hist * SUBLANES
    stride = hrows + PT
    w = xc_ref.shape[1]
    j = pl.program_id(1)

    @pl.when(j == 0)
    def _():
        zcarry_ref[...] = jnp.zeros_like(zcarry_ref)
        hcarry_ref[...] = jnp.zeros_like(hcarry_ref)

    _rmsnorm_to(x_ref, g_ref, h_ref)
    proj = jnp.dot(h_ref[...], win_ref[...], preferred_element_type=F32)
    for st in range(NSUB):
        z_ref[pl.ds(st * stride + hrows, PT), :] = proj[st * PT:(st + 1) * PT, :]
    xa_c, ga_c, hb_c, bg_c, cg_c = (pl.ds(k * w, w) for k in range(5))
    for st in range(NSUB):
        for lanes in (xa_c, hb_c, cg_c):
            _fill_history(z_ref, st * stride, zcarry_ref, hist, lanes)

    def z_row(c):
        return pl.multiple_of(c * ROW_CHUNK + (c // CHUNKS_PER_BLOCK + 1) * hrows, SUBLANES)

    def conv_body(c, carry):
        r0 = pl.multiple_of(c * ROW_CHUNK, ROW_CHUNK)
        z0 = z_row(c)
        acc = cb_ref[...] + z_ref[pl.ds(z0, ROW_CHUNK), xa_c] * cw_ref[3:4, :]
        for k in range(1, 4):
            acc = acc + z_ref[pl.ds(z0 - k * SUBLANES, ROW_CHUNK), xa_c] * cw_ref[3 - k:4 - k, :]
        xc_ref[pl.ds(r0, ROW_CHUNK), :] = acc
        xcb_ref[pl.ds(r0, ROW_CHUNK), :] = acc.astype(BF16)
        return carry
    lax.fori_loop(0, SEQ_TILE // ROW_CHUNK, conv_body, 0)

    for blk in range(w // MXU_DIM):
        lanes = pl.ds(blk * MXU_DIM, MXU_DIM)
        gate_ref[:, pl.ds(blk * MXU_DIM, MXU_DIM)] = jnp.dot(
            xcb_ref[:, lanes], wa_ref[blk], preferred_element_type=F32)
        gate_ref[:, pl.ds(w + blk * MXU_DIM, MXU_DIM)] = jnp.dot(
            xcb_ref[:, lanes], wi_ref[blk], preferred_element_type=F32)

    lam = lam_ref[...]
    neg_c_sp = -LRU_C * (jnp.maximum(-lam, 0.0) + jnp.log1p(jnp.exp(-jnp.abs(lam))))

    def gate_body(c, carry):
        rows = pl.ds(pl.multiple_of(c * ROW_CHUNK, ROW_CHUNK), ROW_CHUNK)
        r = _sigmoid(gate_ref[rows, pl.ds(0, w)] + ba_ref[...])
        ig = _sigmoid(gate_ref[rows, pl.ds(w, w)] + bi_ref[...])
        a = jnp.exp(r * neg_c_sp)
        mult = jnp.sqrt(1.0 - a * a)
        blk = c // CHUNKS_PER_BLOCK
        t0 = j * SEQ_TILE + blk * PT + (c - blk * CHUNKS_PER_BLOCK) * (ROW_CHUNK // SUBLANES)
        mult = jnp.where(_row_time(t0, mult.shape) == 0, 1.0, mult)
        xc_ref[rows, :] = mult * ig * xc_ref[rows, :]
        gate_ref[rows, pl.ds(0, w)] = a
        return carry
    lax.fori_loop(0, SEQ_TILE // ROW_CHUNK, gate_body, 0)

    sub = lax.broadcasted_iota(jnp.int32, (SUBLANES, w), 0)
    for st in range(NSUB):
        def scan_body(i, carry, st=st):
            h, p = carry
            rows = pl.ds(pl.multiple_of(st * PT + i * SUBLANES, SUBLANES), SUBLANES)
            a = gate_ref[rows, pl.ds(0, w)]
            h = a * h + xc_ref[rows, :]
            p = a * p
            xc_ref[rows, :] = h
            gate_ref[rows, pl.ds(0, w)] = p
            return h, p
        zeros = jnp.zeros((SUBLANES, w), F32)
        h_last, p_last = lax.fori_loop(0, SEG, scan_body, (zeros, zeros + 1.0), unroll=8)

        h_in = jnp.where(sub == 0, hcarry_ref[...], 0.0)
        for s in range(SUBLANES - 1):
            nxt = pltpu.roll(h_last + p_last * h_in, 1, axis=0)
            h_in = jnp.where(sub == s + 1, nxt, h_in)
        hcarry_ref[...] = pltpu.roll(h_last + p_last * h_in, 1, axis=0)
        h_rep = jnp.concatenate([h_in] * (ROW_CHUNK // SUBLANES), axis=0)

        def out_body(c, carry, st=st, h_rep=h_rep):
            r0 = pl.multiple_of(st * PT + c * ROW_CHUNK, ROW_CHUNK)
            z0 = pl.multiple_of(st * stride + hrows + c * ROW_CHUNK, SUBLANES)
            rows = pl.ds(r0, ROW_CHUNK)
            h = xc_ref[rows, :] + gate_ref[rows, pl.ds(0, w)] * h_rep
            ya = h * jax.nn.gelu(z_ref[pl.ds(z0, ROW_CHUNK), ga_c])
            conv = None
            for k in range(3):
                zrows = pl.ds(z0 - k * SUBLANES, ROW_CHUNK)
                term = z_ref[zrows, cg_c] * z_ref[zrows, hb_c] * sw_ref[2 - k:3 - k, :]
                conv = term if conv is None else conv + term
            yb = z_ref[pl.ds(z0, ROW_CHUNK), bg_c] * conv
            y_ref[rows, pl.ds(0, w)] = ya.astype(BF16)
            y_ref[rows, pl.ds(w, w)] = yb.astype(BF16)
            return carry
        lax.fori_loop(0, CHUNKS_PER_BLOCK, out_body, 0)

    o_ref[...] = x_ref[...] + jnp.dot(y_ref[...], wout_ref[...], preferred_element_type=F32)


def _even_call(x, g, win, cw, cb, wa, wi, ba, bi, lam, sw, wout):
    bsz, seq, d = x.shape
    zin = win.shape[1]
    w = zin // 5
    full = lambda a: pl.BlockSpec(a.shape, lambda b, j: (0,) * a.ndim)
    xspec = pl.BlockSpec((None, SEQ_TILE, d), lambda b, j: (b, j, 0))
    args = (x, g, win, cw, cb, wa, wi, ba, bi, lam, sw, wout)
    return pl.pallas_call(
        _even_kernel,
        out_shape=jax.ShapeDtypeStruct(x.shape, x.dtype),
        grid=(bsz, seq // SEQ_TILE),
        in_specs=[xspec] + [full(a) for a in args[1:]],
        out_specs=xspec,
        scratch_shapes=[
            pltpu.VMEM((SEQ_TILE, d), BF16),
            pltpu.VMEM((NSUB * (PT + 3 * SUBLANES), zin), F32),
            pltpu.VMEM((SEQ_TILE, w), F32),
            pltpu.VMEM((SEQ_TILE, w), BF16),
            pltpu.VMEM((SEQ_TILE, 2 * w), F32),
            pltpu.VMEM((SEQ_TILE, 2 * w), BF16),
            pltpu.VMEM((3 * SUBLANES, zin), F32),
            pltpu.VMEM((SUBLANES, w), F32),
        ],
        compiler_params=_cparams(),
        name="lru_sconv_mixer",
    )(*args)


def _pair_levels():
    p = np.arange(PT)
    t = (p % SUBLANES) * SEG + p // SUBLANES
    x = t[:, None] ^ t[None, :]
    lvl = np.full(x.shape, -1.0, np.float32)
    for b in range(TIME_BITS):
        lvl[(x >> b) == 1] = b
    return lvl


def _sublane_row(v, s):
    return jnp.broadcast_to(v[s:s + 1, :], v.shape)


def _odd_kernel(x_ref, g_ref, win_ref, pw_ref, pscale_ref, lbl_ref, ng_ref, lvl_ref, wout_ref, o_ref,
                h_ref, z_ref, pa_ref, pb_ref, pcarry_ref, cum_ref, k_ref, lvq_ref, lvk_ref, qh_ref, kh_ref,
                vb_ref, att_ref, oh_ref, y_ref, state_ref, *, layer):
    w = pscale_ref.shape[1]
    heads = w // LANES
    j = pl.program_id(1)
    hist = POOL_WINDOWS[-1] // 2
    hrows = hist * SUBLANES

    @pl.when(jnp.logical_and(pl.program_id(0) == 0, j == 0))
    def _():
        lvq_ref[...] = jnp.zeros_like(lvq_ref)
        lvk_ref[...] = jnp.zeros_like(lvk_ref)

    @pl.when(j == 0)
    def _():
        pcarry_ref[...] = jnp.zeros_like(pcarry_ref)
        state_ref[...] = jnp.zeros_like(state_ref)

    _rmsnorm_to(x_ref, g_ref, h_ref)
    z_ref[...] = jnp.dot(h_ref[...], win_ref[...], preferred_element_type=F32)
    uc_c, q_c, fz_c, v_c, gd_c = (k * w for k in range(5))

    logits = lbl_ref[...]
    sm = jnp.exp(logits - jnp.max(logits, axis=0, keepdims=True))
    sm = sm / jnp.sum(sm, axis=0, keepdims=True)
    lb = jnp.zeros((1, w), F32)
    for o in range(1, layer + 1):
        lb = lb + sm[o:o + 1, :]
    log_lb = jnp.log(lb)
    log_1m_lb = jnp.log1p(-lb)

    sub8 = lax.broadcasted_iota(jnp.int32, (SUBLANES, LANES), 0)
    row16 = lax.broadcasted_iota(jnp.int32, (PAIR_ROWS, LANES), 0)
    sub16 = row16 & (SUBLANES - 1)
    nt = (((1,), (1,)), ((), ()))
    tn = (((0,), (0,)), ((), ()))
    twice = lambda v: jnp.concatenate([v, v], axis=0)
    bufs = (pa_ref, pb_ref)

    for st in range(NSUB):
        base = st * PT
        t_base = j * SEQ_TILE + base

        pa_ref[pl.ds(hrows, PT), :] = z_ref[pl.ds(base, PT), pl.ds(uc_c, w)]
        for stage in range(len(POOL_WINDOWS)):
            shift = 1 << stage
            src = bufs[stage % 2]
            dst = bufs[(stage + 1) % 2]
            lanes = pl.ds(stage * LANES, w - stage * LANES)
            for gidx in range(shift):
                tail = src[pl.ds(hrows + PT - (shift - gidx) * SUBLANES, SUBLANES), lanes]
                rolled = pltpu.roll(tail, 1, axis=0)
                crow = pl.ds((shift - 1 + gidx) * SUBLANES, SUBLANES)
                prev = pcarry_ref[crow, lanes]
                src[pl.ds(hrows - (shift - gidx) * SUBLANES, SUBLANES), lanes] = jnp.where(
                    lax.broadcasted_iota(jnp.int32, rolled.shape, 0) == 0, prev, rolled)
                pcarry_ref[crow, lanes] = rolled
            for r in range(CHUNKS_PER_BLOCK):
                r0 = r * ROW_CHUNK + hrows
                dst[pl.ds(r0, ROW_CHUNK), lanes] = (
                    src[pl.ds(r0, ROW_CHUNK), lanes] + src[pl.ds(r0 - shift * SUBLANES, ROW_CHUNK), lanes])

        for r in range(CHUNKS_PER_BLOCK):
            t = _row_time(t_base + r * (ROW_CHUNK // SUBLANES), (ROW_CHUNK, LANES))
            for gi, win in enumerate(POOL_WINDOWS):
                buf = bufs[(gi + 1) % 2]
                lanes = pl.ds(gi * LANES, LANES)
                cnt = jnp.minimum(t + 1, win).astype(F32)
                rows = pl.ds(base + r * ROW_CHUNK, ROW_CHUNK)
                p = buf[pl.ds(hrows + r * ROW_CHUNK, ROW_CHUNK), lanes] / cnt - z_ref[rows, pl.ds(uc_c + gi * LANES, LANES)]
                y_ref[rows, lanes] = p.astype(BF16)

        for hd in range(heads):
            par = hd % 2
            cum_b, k_b, lvq, lvk = cum_ref.at[par], k_ref.at[par], lvq_ref.at[par], lvk_ref.at[par]
            qh_b, kh_b, vb_b, att_b = qh_ref.at[par], kh_ref.at[par], vb_ref.at[par], att_ref.at[par]
            la = log_lb[:, hd * LANES:(hd + 1) * LANES]
            l1 = log_1m_lb[:, hd * LANES:(hd + 1) * LANES]
            om = 1.0 - lb[:, hd * LANES:(hd + 1) * LANES]
            q_l = pl.ds(q_c + hd * LANES, LANES)
            fz_l = pl.ds(fz_c + hd * LANES, LANES)
            v_l = pl.ds(v_c + hd * LANES, LANES)
            blk_rows = pl.ds(base, PT)

            cum = jnp.zeros((SUBLANES, LANES), F32)
            for r in range(CHUNKS_PER_BLOCK):
                fz = z_ref[pl.ds(base + r * ROW_CHUNK, ROW_CHUNK), fz_l]
                e = jnp.exp(-jnp.abs(fz))
                log_sig = jnp.minimum(fz, 0.0) - jnp.log1p(e)
                b = l1 + log_sig
                log_f = jnp.maximum(la, b) + jnp.log1p(jnp.exp(-jnp.abs(la - b)))
                rcp = 1.0 / (1.0 + e)
                k_b[pl.ds(r * ROW_CHUNK, ROW_CHUNK), :] = om * jnp.where(fz >= 0, e * rcp, rcp)
                parts = []
                for g in range(ROW_CHUNK // SUBLANES):
                    cum = cum + log_f[g * SUBLANES:(g + 1) * SUBLANES, :]
                    parts.append(cum)
                cum_b[pl.ds(r * ROW_CHUNK, ROW_CHUNK), :] = jnp.concatenate(parts, axis=0)
            cum_last = cum

            incl = cum_last
            for sh in (1, 2, 4):
                incl = incl + jnp.where(sub8 >= sh, pltpu.roll(incl, sh, axis=0), 0.0)
            offset = incl - cum_last
            total = _sublane_row(incl, SUBLANES - 1)

            seg_levels = []
            for c in range(TIME_BITS - SEG_BITS):
                upper8 = ((sub8 >> c) & 1) == 1
                incl_mid = jnp.zeros_like(incl)
                for s_mid in sorted({((s >> (c + 1)) << (c + 1)) | ((1 << c) - 1) for s in range(SUBLANES)}):
                    in_block = (sub8 >> (c + 1)) == (s_mid >> (c + 1))
                    incl_mid = jnp.where(in_block, _sublane_row(incl, s_mid), incl_mid)
                const = jnp.where(upper8, offset - incl_mid, incl_mid - incl + cum_last)
                seg_levels.append((((sub16 >> c) & 1) == 1, twice(const)))
            q_const = twice(offset)
            k_const = twice(cum_last + total - incl)

            for m in range(SEG // 2):
                rows = pl.ds(m * PAIR_ROWS, PAIR_ROWS)
                q = z_ref[pl.ds(base + m * PAIR_ROWS, PAIR_ROWS), q_l]
                k = k_b[rows, :]
                cum = cum_b[rows, :]
                for b in range(SEG_BITS):
                    if b == 0:
                        upper = row16 >= SUBLANES
                        dec = jnp.exp(cum - twice(cum[0:SUBLANES, :]))
                        lvq[0, rows, :] = jnp.where(upper, q * dec, 0.0).astype(BF16)
                        lvk[0, rows, :] = jnp.where(upper, 0.0, k).astype(BF16)
                        continue
                    mid = ((m >> b) << (b + 1)) | ((1 << b) - 1)
                    ref = twice(cum_b[pl.ds(mid * SUBLANES, SUBLANES), :])
                    if (m >> (b - 1)) & 1:
                        lvq[b, rows, :] = (q * jnp.exp(cum - ref)).astype(BF16)
                    else:
                        lvk[b, rows, :] = (k * jnp.exp(ref - cum)).astype(BF16)
                ncum = -cum
                for b in range(SEG_BITS, TIME_BITS):
                    upper, const = seg_levels[b - SEG_BITS]
                    wgt = jnp.where(upper, q, k) * jnp.exp(jnp.where(upper, cum, ncum) + const)
                    lvq[b, rows, :] = jnp.where(upper, wgt, 0.0).astype(BF16)
                    lvk[b, rows, :] = jnp.where(upper, 0.0, wgt).astype(BF16)
                qh_b[rows, :] = (q * jnp.exp(cum + q_const)).astype(BF16)
                kh_b[rows, :] = (k * jnp.exp(ncum + k_const)).astype(BF16)

            vb_b[...] = z_ref[blk_rows, v_l].astype(BF16)
            for b in range(TIME_BITS):
                part = lax.dot_general(lvq[b], lvk[b], nt, preferred_element_type=F32)
                hit = lvl_ref[...] == float(b)
                att_b[...] = jnp.where(hit, part, 0.0 if b == 0 else att_b[...])

            state = state_ref[hd]
            diag = jnp.sum(z_ref[blk_rows, q_l] * k_b[...], axis=-1, keepdims=True)
            o = jnp.dot(att_b[...].astype(BF16), vb_b[...], preferred_element_type=F32)
            o = o + lax.dot_general(qh_b[...], state.astype(BF16), nt, preferred_element_type=F32)
            oh_ref[hd, blk_rows, :] = o + diag * z_ref[blk_rows, v_l]
            upd = lax.dot_general(vb_b[...], kh_b[...], tn, preferred_element_type=F32)
            state_ref[hd] = state * jnp.exp(total[0:1, :]) + upd

    for blk in range(w // MXU_DIM):
        lanes = pl.ds(blk * MXU_DIM, MXU_DIM)
        mixed = jnp.dot(y_ref[:, lanes], pw_ref[blk], preferred_element_type=F32)
        y_ref[:, lanes] = (mixed * pscale_ref[:, lanes]).astype(BF16)

    for r in range(SEQ_TILE // ROW_CHUNK):
        rows = pl.ds(r * ROW_CHUNK, ROW_CHUNK)
        for hd in range(heads):
            o = oh_ref[hd, rows, :]
            o = o * lax.rsqrt(jnp.mean(o * o, axis=-1, keepdims=True) + EPS) * ng_ref[:, pl.ds(hd * LANES, LANES)]
            gd = z_ref[rows, pl.ds(gd_c + hd * LANES, LANES)]
            y_ref[rows, pl.ds(w + hd * LANES, LANES)] = (o * (gd * _sigmoid(gd))).astype(BF16)

    o_ref[...] = x_ref[...] + jnp.dot(y_ref[...], wout_ref[...], preferred_element_type=F32)


def _odd_call(x, g, win, pw, pscale, lbl, ng, levels, wout, layer):
    bsz, seq, d = x.shape
    zin = win.shape[1]
    w = zin // 5
    heads = w // LANES
    hist = POOL_WINDOWS[-1] // 2
    full = lambda a: pl.BlockSpec(a.shape, lambda b, j: (0,) * a.ndim)
    xspec = pl.BlockSpec((None, SEQ_TILE, d), lambda b, j: (b, j, 0))
    args = (x, g, win, pw, pscale, lbl, ng, levels, wout)
    return pl.pallas_call(
        functools.partial(_odd_kernel, layer=layer),
        out_shape=jax.ShapeDtypeStruct(x.shape, x.dtype),
        grid=(bsz, seq // SEQ_TILE),
        in_specs=[xspec] + [full(a) for a in args[1:]],
        out_specs=xspec,
        scratch_shapes=[
            pltpu.VMEM((SEQ_TILE, d), BF16),
            pltpu.VMEM((SEQ_TILE, zin), F32),
            pltpu.VMEM((PT + hist * SUBLANES, w), F32),
            pltpu.VMEM((PT + hist * SUBLANES, w), F32),
            pltpu.VMEM(((POOL_WINDOWS[-1] - 1) * SUBLANES, w), F32),
            pltpu.VMEM((2, PT, LANES), F32),
            pltpu.VMEM((2, PT, LANES), F32),
            pltpu.VMEM((2, TIME_BITS, PT, LANES), BF16),
            pltpu.VMEM((2, TIME_BITS, PT, LANES), BF16),
            pltpu.VMEM((2, PT, LANES), BF16),
            pltpu.VMEM((2, PT, LANES), BF16),
            pltpu.VMEM((2, PT, LANES), BF16),
            pltpu.VMEM((2, PT, PT), F32),
            pltpu.VMEM((heads, SEQ_TILE, LANES), F32),
            pltpu.VMEM((SEQ_TILE, 2 * w), BF16),
            pltpu.VMEM((heads, LANES, LANES), F32),
        ],
        compiler_params=_cparams(),
        name="pool_hgrn_mixer",
    )(*args)


def _to_segment_major(x):
    b, s, d = x.shape
    return x.reshape(b, s // PT, SUBLANES, SEG, d).transpose(0, 1, 3, 2, 4).reshape(b, s, d)


def _from_segment_major(x):
    b, s, d = x.shape
    return x.reshape(b, s // PT, SEG, SUBLANES, d).transpose(0, 1, 3, 2, 4).reshape(b, s, d)


def _block_diag_tiles(wblocks):
    nb, n, _ = wblocks.shape
    per = MXU_DIM // n
    tiles = []
    for t in range(nb // per):
        tile = jnp.zeros((MXU_DIM, MXU_DIM), wblocks.dtype)
        for k in range(per):
            tile = lax.dynamic_update_slice(tile, wblocks[t * per + k], (k * n, k * n))
        tiles.append(tile)
    return jnp.stack(tiles).astype(BF16)


def kernel(x, g_mix, g_ffn, g_final, w_in_even, w_out_even, lru_conv_w, lru_conv_b, lru_wa, lru_ba, lru_wi, lru_bi, lru_lambda, sconv_w, w_in_odd, w_out_odd, pool_w, pool_scale, hgrn_lb_logits, hgrn_norm_g, ffn_w_up, ffn_w_gate, ffn_conv_w, ffn_conv_b, ffn_w_down):
    depth = g_mix.shape[0]
    assert x.shape[1] % SEQ_TILE == 0
    row = lambda v: v.reshape(1, -1).astype(F32)
    levels = jnp.asarray(_pair_levels())
    x = _to_segment_major(x)
    for l in range(depth):
        if l % 2 == 0:
            e = l // 2
            x = _even_call(
                x, row(g_mix[l]), w_in_even[e].astype(BF16), lru_conv_w[e], row(lru_conv_b[e]),
                _block_diag_tiles(lru_wa[e]), _block_diag_tiles(lru_wi[e]), row(lru_ba[e]), row(lru_bi[e]),
                row(lru_lambda[e]), sconv_w[e], w_out_even[e].astype(BF16))
        else:
            o = l // 2
            x = _odd_call(
                x, row(g_mix[l]), w_in_odd[o].astype(BF16), _block_diag_tiles(pool_w[o]), row(pool_scale[o]),
                hgrn_lb_logits.astype(F32), row(hgrn_norm_g[o]), levels, w_out_odd[o].astype(BF16), o)
        x = _ffn_call(
            x, row(g_ffn[l]), ffn_w_up[l].astype(BF16), ffn_w_gate[l].astype(BF16), ffn_conv_w[l],
            row(ffn_conv_b[l]), ffn_w_down[l].astype(BF16), row(g_final), l == depth - 1)
    return _from_segment_major(x)
```

```python
import functools

import jax
import jax.numpy as jnp
import numpy as np
from jax import lax
from jax.experimental import pallas as pl
from jax.experimental.pallas import tpu as pltpu

F32 = jnp.float32
BF16 = jnp.bfloat16

SUBLANES = 8
LANES = 128
MXU_DIM = 256
VMEM_LIMIT_BYTES = 56 * 1024 * 1024

EPS = 1e-6
LRU_C = 8.0
POOL_WINDOWS = (2, 4, 8, 16)

SEQ_TILE = 512
PT = 256
NSUB = SEQ_TILE // PT
SEG = PT // SUBLANES
SEG_BITS = SEG.bit_length() - 1
TIME_BITS = PT.bit_length() - 1
ROW_CHUNK = 64
CHUNKS_PER_BLOCK = PT // ROW_CHUNK
FFN_CHUNK = 256
PAIR_ROWS = 2 * SUBLANES
CAST_ROWS = 512


def _cparams():
    return pltpu.CompilerParams(
        dimension_semantics=("arbitrary", "arbitrary"),
        vmem_limit_bytes=VMEM_LIMIT_BYTES,
    )


def _sigmoid(v):
    return 0.5 * jnp.tanh(0.5 * v) + 0.5


def _rmsnorm_rows(x, g):
    ms = jnp.mean(x * x, axis=-1, keepdims=True)
    return x * lax.rsqrt(ms + EPS) * g


def _rmsnorm_to(x_ref, g_ref, h_ref, row0=0, nrows=SEQ_TILE):
    for r in range(nrows // ROW_CHUNK):
        rows = pl.ds(row0 + r * ROW_CHUNK, ROW_CHUNK)
        h_ref[rows, :] = _rmsnorm_rows(x_ref[rows, :], g_ref[...]).astype(h_ref.dtype)


def _fill_history(buf_ref, base, carry_ref, groups, lanes, carry_lanes=None):
    carry_lanes = lanes if carry_lanes is None else carry_lanes
    for g in range(groups):
        tail = buf_ref[pl.ds(base + PT + g * SUBLANES, SUBLANES), lanes]
        rolled = pltpu.roll(tail, 1, axis=0)
        sub = lax.broadcasted_iota(jnp.int32, rolled.shape, 0)
        prev = carry_ref[pl.ds(g * SUBLANES, SUBLANES), carry_lanes]
        buf_ref[pl.ds(base + g * SUBLANES, SUBLANES), lanes] = jnp.where(sub == 0, prev, rolled)
        carry_ref[pl.ds(g * SUBLANES, SUBLANES), carry_lanes] = rolled


def _row_time(t0, shape):
    row = lax.broadcasted_iota(jnp.int32, shape, 0)
    sub = row & (SUBLANES - 1)
    grp = row >> (SUBLANES.bit_length() - 1)
    return t0 + sub * SEG + grp


def _layer_spec(a, l):
    return pl.BlockSpec((None,) + a.shape[1:], lambda b, j: (l,) + (0,) * (a.ndim - 1))


def _whole_spec(a):
    return pl.BlockSpec(a.shape, lambda b, j: (0,) * a.ndim)


def _ffn_kernel(x_ref, g_ref, wup_ref, wgate_ref, cw_ref, cb_ref, wdown_ref, gfin_ref, o_ref,
                h_ref, up_ref, gate_ref, act_ref, carry_ref, *, final_norm):
    hist = 2
    hrows = hist * SUBLANES
    stride = hrows + PT
    ffn_dim = wup_ref.shape[1]

    @pl.when(pl.program_id(1) == 0)
    def _():
        carry_ref[...] = jnp.zeros_like(carry_ref)

    _rmsnorm_to(x_ref, g_ref, h_ref)

    for c in range(ffn_dim // FFN_CHUNK):
        cols = pl.ds(c * FFN_CHUNK, FFN_CHUNK)
        up = up_ref.at[c % 2]
        gate = gate_ref.at[c % 2]
        proj = jnp.dot(h_ref[...], wup_ref[:, cols], preferred_element_type=F32)
        for st in range(NSUB):
            up[pl.ds(st * stride + hrows, PT), :] = proj[st * PT:(st + 1) * PT, :]
        gate[...] = jnp.dot(h_ref[...], wgate_ref[:, cols], preferred_element_type=F32)
        w0 = cw_ref[0:1, cols]
        w1 = cw_ref[1:2, cols]
        w2 = cw_ref[2:3, cols]
        b = cb_ref[:, cols]
        for st in range(NSUB):
            _fill_history(up, st * stride, carry_ref, hist, slice(None), cols)
            for r in range(CHUNKS_PER_BLOCK):
                r0 = st * stride + r * ROW_CHUNK
                u = (up[pl.ds(r0 + hrows, ROW_CHUNK), :] * w2
                     + up[pl.ds(r0 + hrows - SUBLANES, ROW_CHUNK), :] * w1
                     + up[pl.ds(r0, ROW_CHUNK), :] * w0 + b)
                rows = pl.ds(st * PT + r * ROW_CHUNK, ROW_CHUNK)
                act_ref[rows, cols] = (jax.nn.gelu(u) * gate[rows, :]).astype(BF16)

    o_ref[...] = x_ref[...] + jnp.dot(act_ref[...], wdown_ref[...], preferred_element_type=F32)
    if final_norm:
        for r in range(SEQ_TILE // ROW_CHUNK):
            rows = pl.ds(r * ROW_CHUNK, ROW_CHUNK)
            o_ref[rows, :] = _rmsnorm_rows(o_ref[rows, :], gfin_ref[...])


def _ffn_call(x, params, layer, final_norm):
    bsz, seq, d = x.shape
    g, wup, wgate, cw, cb, wdown, gfin = params
    f = wup.shape[-1]
    xspec = pl.BlockSpec((None, SEQ_TILE, d), lambda b, j: (b, j, 0))
    return pl.pallas_call(
        functools.partial(_ffn_kernel, final_norm=final_norm),
        out_shape=jax.ShapeDtypeStruct(x.shape, x.dtype),
        grid=(bsz, seq // SEQ_TILE),
        in_specs=[xspec] + [_layer_spec(a, layer) for a in params[:-1]] + [_whole_spec(gfin)],
        out_specs=xspec,
        scratch_shapes=[
            pltpu.VMEM((SEQ_TILE, d), BF16),
            pltpu.VMEM((2, NSUB * (PT + 2 * SUBLANES), FFN_CHUNK), F32),
            pltpu.VMEM((2, SEQ_TILE, FFN_CHUNK), F32),
            pltpu.VMEM((SEQ_TILE, f), BF16),
            pltpu.VMEM((2 * SUBLANES, f), F32),
        ],
        compiler_params=_cparams(),
        name="conv_ffn",
    )(x, *params)


def _even_kernel(x_ref, g_ref, win_ref, cw_ref, cb_ref, wa_ref, wi_ref, ba_ref, bi_ref, lam_ref,
                 sw_ref, wout_ref, o_ref,
                 h_ref, *scratch):
    z_refs = scratch[:NSUB]
    xc_ref, xcb_ref, gate_ref, y_ref, zcarry_ref, hcarry_ref = scratch[NSUB:]
    hist = 3
    hrows = hist * SUBLANES
    w = xc_ref.shape[1]
    j = pl.program_id(1)

    @pl.when(j == 0)
    def _():
        zcarry_ref[...] = jnp.zeros_like(zcarry_ref)
        hcarry_ref[...] = jnp.zeros_like(hcarry_ref)

    xa_c, ga_c, hb_c, bg_c, cg_c = (pl.ds(k * w, w) for k in range(5))
    r_l, i_l = pl.ds(0, w), pl.ds(w, w)
    sub = lax.broadcasted_iota(jnp.int32, (SUBLANES, w), 0)

    lam = lam_ref[...]
    neg_c_sp = -LRU_C * (jnp.maximum(-lam, 0.0) + jnp.log1p(jnp.exp(-jnp.abs(lam))))

    blk_rows = lambda st: pl.ds(st * PT, PT)
    chunk_rows = lambda st, r: pl.ds(st * PT + r * ROW_CHUNK, ROW_CHUNK)
    h_rep = {}

    def project(st, k):
        cols = pl.ds(k * w, w)
        z_refs[st][pl.ds(hrows, PT), cols] = jnp.dot(
            h_ref[blk_rows(st), :], win_ref[:, cols], preferred_element_type=F32)

    def history(st):
        for lanes in (xa_c, hb_c, cg_c):
            _fill_history(z_refs[st], 0, zcarry_ref, hist, lanes)

    def lru_conv(st, r):
        z_ref = z_refs[st]
        z0 = hrows + r * ROW_CHUNK
        acc = cb_ref[...] + z_ref[pl.ds(z0, ROW_CHUNK), xa_c] * cw_ref[3:4, :]
        for k in range(1, 4):
            acc = acc + z_ref[pl.ds(z0 - k * SUBLANES, ROW_CHUNK), xa_c] * cw_ref[3 - k:4 - k, :]
        xc_ref[chunk_rows(st, r), :] = acc
        xcb_ref[chunk_rows(st, r), :] = acc.astype(BF16)

    def gate_matmuls(st):
        for blk in range(w // MXU_DIM):
            lanes = pl.ds(blk * MXU_DIM, MXU_DIM)
            gate_ref[blk_rows(st), pl.ds(blk * MXU_DIM, MXU_DIM)] = jnp.dot(
                xcb_ref[blk_rows(st), lanes], wa_ref[blk], preferred_element_type=F32)
            gate_ref[blk_rows(st), pl.ds(w + blk * MXU_DIM, MXU_DIM)] = jnp.dot(
                xcb_ref[blk_rows(st), lanes], wi_ref[blk], preferred_element_type=F32)

    def gates(st, r):
        rows = chunk_rows(st, r)
        rg = _sigmoid(gate_ref[rows, r_l] + ba_ref[...])
        ig = _sigmoid(gate_ref[rows, i_l] + bi_ref[...])
        a = jnp.exp(rg * neg_c_sp)
        mult = jnp.sqrt(1.0 - a * a)
        t = _row_time(j * SEQ_TILE + st * PT + r * (ROW_CHUNK // SUBLANES), mult.shape)
        mult = jnp.where(t == 0, 1.0, mult)
        xc_ref[rows, :] = mult * ig * xc_ref[rows, :]
        gate_ref[rows, r_l] = a

    def scan(st):
        h = jnp.zeros((SUBLANES, w), F32)
        p = h + 1.0
        for i in range(SEG):
            rows = pl.ds(st * PT + i * SUBLANES, SUBLANES)
            a = gate_ref[rows, r_l]
            h = a * h + xc_ref[rows, :]
            p = a * p
            xc_ref[rows, :] = h
            gate_ref[rows, r_l] = p
        h_in = jnp.where(sub == 0, hcarry_ref[...], 0.0)
        for s in range(SUBLANES - 1):
            nxt = pltpu.roll(h + p * h_in, 1, axis=0)
            h_in = jnp.where(sub == s + 1, nxt, h_in)
        hcarry_ref[...] = pltpu.roll(h + p * h_in, 1, axis=0)
        h_rep[st] = jnp.concatenate([h_in] * (ROW_CHUNK // SUBLANES), axis=0)

    def outputs(st, r):
        z_ref = z_refs[st]
        rows = chunk_rows(st, r)
        z0 = hrows + r * ROW_CHUNK
        hs = xc_ref[rows, :] + gate_ref[rows, r_l] * h_rep[st]
        ya = hs * jax.nn.gelu(z_ref[pl.ds(z0, ROW_CHUNK), ga_c])
        conv = None
        for k in range(3):
            zrows = pl.ds(z0 - k * SUBLANES, ROW_CHUNK)
            term = z_ref[zrows, cg_c] * z_ref[zrows, hb_c] * sw_ref[2 - k:3 - k, :]
            conv = term if conv is None else conv + term
        yb = z_ref[pl.ds(z0, ROW_CHUNK), bg_c] * conv
        y_ref[rows, pl.ds(0, w)] = ya.astype(BF16)
        y_ref[rows, pl.ds(w, w)] = yb.astype(BF16)

    def out_project(st, n):
        cols = pl.ds(n * MXU_DIM, MXU_DIM)
        o_ref[blk_rows(st), cols] = x_ref[blk_rows(st), cols] + jnp.dot(
            y_ref[blk_rows(st), :], wout_ref[:, cols], preferred_element_type=F32)

    d_model = x_ref.shape[1]
    n_out = d_model // MXU_DIM
    _rmsnorm_to(x_ref, g_ref, h_ref, 0, PT)
    for k in range(5):
        project(0, k)
    for st in range(NSUB):
        nxt = st + 1 if st + 1 < NSUB else None
        prv = st - 1 if st > 0 else None
        pieces = []
        if nxt is not None:
            pieces += [functools.partial(project, nxt, k) for k in range(5)]
        if prv is not None:
            pieces += [functools.partial(out_project, prv, n) for n in range(n_out)]
        vector_steps = ([functools.partial(lru_conv, st, r) for r in range(CHUNKS_PER_BLOCK)]
                        + [functools.partial(gate_matmuls, st)]
                        + [functools.partial(gates, st, r) for r in range(CHUNKS_PER_BLOCK)]
                        + [functools.partial(scan, st)]
                        + [functools.partial(outputs, st, r) for r in range(CHUNKS_PER_BLOCK)])
        history(st)
        if nxt is not None:
            _rmsnorm_to(x_ref, g_ref, h_ref, nxt * PT, PT)
        emitted = 0
        for idx, step in enumerate(vector_steps):
            step()
            due = (idx + 1) * len(pieces) // len(vector_steps)
            while emitted < due:
                pieces[emitted]()
                emitted += 1
    for n in range(n_out):
        out_project(NSUB - 1, n)


def _even_call(x, params, layer):
    bsz, seq, d = x.shape
    zin = params[1].shape[-1]
    w = zin // 5
    xspec = pl.BlockSpec((None, SEQ_TILE, d), lambda b, j: (b, j, 0))
    specs = [_layer_spec(params[0], layer)] + [_layer_spec(a, layer // 2) for a in params[1:]]
    return pl.pallas_call(
        _even_kernel,
        out_shape=jax.ShapeDtypeStruct(x.shape, x.dtype),
        grid=(bsz, seq // SEQ_TILE),
        in_specs=[xspec] + specs,
        out_specs=xspec,
        scratch_shapes=[
            pltpu.VMEM((SEQ_TILE, d), BF16),
        ] + [pltpu.VMEM((PT + 3 * SUBLANES, zin), F32) for _ in range(NSUB)] + [
            pltpu.VMEM((SEQ_TILE, w), F32),
            pltpu.VMEM((SEQ_TILE, w), BF16),
            pltpu.VMEM((SEQ_TILE, 2 * w), F32),
            pltpu.VMEM((SEQ_TILE, 2 * w), BF16),
            pltpu.VMEM((3 * SUBLANES, zin), F32),
            pltpu.VMEM((SUBLANES, w), F32),
        ],
        compiler_params=_cparams(),
        name="lru_sconv_mixer",
    )(x, *params)


def _pair_levels():
    p = np.arange(PT)
    t = (p % SUBLANES) * SEG + p // SUBLANES
    x = t[:, None] ^ t[None, :]
    lvl = np.full(x.shape, -1.0, np.float32)
    for b in range(TIME_BITS):
        lvl[(x >> b) == 1] = b
    return lvl


def _sublane_row(v, s):
    return jnp.broadcast_to(v[s:s + 1, :], v.shape)


def _odd_kernel(x_ref, g_ref, win_ref, pw_ref, pscale_ref, lbl_ref, ng_ref, lvl_ref, wout_ref, o_ref,
                h_ref, z_ref, pa_ref, pb_ref, pcarry_ref, cum_ref, k_ref, lvq_ref, lvk_ref, qh_ref, kh_ref,
                vb_ref, att_ref, oh_ref, y_ref, state_ref, *, layer):
    w = pscale_ref.shape[1]
    heads = w // LANES
    j = pl.program_id(1)
    hist = POOL_WINDOWS[-1] // 2
    hrows = hist * SUBLANES

    @pl.when(jnp.logical_and(pl.program_id(0) == 0, j == 0))
    def _():
        lvq_ref[...] = jnp.zeros_like(lvq_ref)
        lvk_ref[...] = jnp.zeros_like(lvk_ref)

    @pl.when(j == 0)
    def _():
        pcarry_ref[...] = jnp.zeros_like(pcarry_ref)
        state_ref[...] = jnp.zeros_like(state_ref)

    uc_c, q_c, fz_c, v_c, gd_c = (k * w for k in range(5))

    logits = lbl_ref[...]
    sm = jnp.exp(logits - jnp.max(logits, axis=0, keepdims=True))
    sm = sm / jnp.sum(sm, axis=0, keepdims=True)
    lb = jnp.zeros((1, w), F32)
    for o in range(1, layer + 1):
        lb = lb + sm[o:o + 1, :]
    log_lb = jnp.log(lb)
    log_1m_lb = jnp.log1p(-lb)

    sub8 = lax.broadcasted_iota(jnp.int32, (SUBLANES, LANES), 0)
    row16 = lax.broadcasted_iota(jnp.int32, (PAIR_ROWS, LANES), 0)
    sub16 = row16 & (SUBLANES - 1)
    nt = (((1,), (1,)), ((), ()))
    tn = (((0,), (0,)), ((), ()))
    twice = lambda v: jnp.concatenate([v, v], axis=0)
    bufs = (pa_ref, pb_ref)

    blk_rows = lambda st: pl.ds(st * PT, PT)

    def project(st, k):
        cols = pl.ds(k * w, w)
        z_ref[blk_rows(st), cols] = jnp.dot(h_ref[blk_rows(st), :], win_ref[:, cols], preferred_element_type=F32)

    def out_project(st, n):
        cols = pl.ds(n * MXU_DIM, MXU_DIM)
        o_ref[blk_rows(st), cols] = x_ref[blk_rows(st), cols] + jnp.dot(
            y_ref[blk_rows(st), :], wout_ref[:, cols], preferred_element_type=F32)

    def pool(st):
        base = st * PT
        pa_ref[pl.ds(hrows, PT), :] = z_ref[blk_rows(st), pl.ds(uc_c, w)]
        for stage in range(len(POOL_WINDOWS)):
            shift = 1 << stage
            src = bufs[stage % 2]
            dst = bufs[(stage + 1) % 2]
            lanes = pl.ds(stage * LANES, w - stage * LANES)
            for gidx in range(shift):
                tail = src[pl.ds(hrows + PT - (shift - gidx) * SUBLANES, SUBLANES), lanes]
                rolled = pltpu.roll(tail, 1, axis=0)
                crow = pl.ds((shift - 1 + gidx) * SUBLANES, SUBLANES)
                prev = pcarry_ref[crow, lanes]
                src[pl.ds(hrows - (shift - gidx) * SUBLANES, SUBLANES), lanes] = jnp.where(
                    lax.broadcasted_iota(jnp.int32, rolled.shape, 0) == 0, prev, rolled)
                pcarry_ref[crow, lanes] = rolled
            for r in range(CHUNKS_PER_BLOCK):
                r0 = r * ROW_CHUNK + hrows
                dst[pl.ds(r0, ROW_CHUNK), lanes] = (
                    src[pl.ds(r0, ROW_CHUNK), lanes] + src[pl.ds(r0 - shift * SUBLANES, ROW_CHUNK), lanes])

        for r in range(CHUNKS_PER_BLOCK):
            t = _row_time(j * SEQ_TILE + base + r * (ROW_CHUNK // SUBLANES), (ROW_CHUNK, LANES))
            for gi, win in enumerate(POOL_WINDOWS):
                buf = bufs[(gi + 1) % 2]
                lanes = pl.ds(gi * LANES, LANES)
                cnt = jnp.minimum(t + 1, win).astype(F32)
                rows = pl.ds(base + r * ROW_CHUNK, ROW_CHUNK)
                p = buf[pl.ds(hrows + r * ROW_CHUNK, ROW_CHUNK), lanes] / cnt - z_ref[rows, pl.ds(uc_c + gi * LANES, LANES)]
                y_ref[rows, lanes] = p.astype(BF16)

        for blk in range(w // MXU_DIM):
            lanes = pl.ds(blk * MXU_DIM, MXU_DIM)
            mixed = jnp.dot(y_ref[blk_rows(st), lanes], pw_ref[blk], preferred_element_type=F32)
            y_ref[blk_rows(st), lanes] = (mixed * pscale_ref[:, lanes]).astype(BF16)

    totals = {}

    def head_factors(st, hd):
        base = st * PT
        par = hd % 2
        cum_b, k_b, lvq, lvk = cum_ref.at[par], k_ref.at[par], lvq_ref.at[par], lvk_ref.at[par]
        qh_b, kh_b, vb_b = qh_ref.at[par], kh_ref.at[par], vb_ref.at[par]
        la = log_lb[:, hd * LANES:(hd + 1) * LANES]
        l1 = log_1m_lb[:, hd * LANES:(hd + 1) * LANES]
        om = 1.0 - lb[:, hd * LANES:(hd + 1) * LANES]
        q_l = pl.ds(q_c + hd * LANES, LANES)
        fz_l = pl.ds(fz_c + hd * LANES, LANES)
        v_l = pl.ds(v_c + hd * LANES, LANES)

        cum = jnp.zeros((SUBLANES, LANES), F32)
        for r in range(CHUNKS_PER_BLOCK):
            fz = z_ref[pl.ds(base + r * ROW_CHUNK, ROW_CHUNK), fz_l]
            e = jnp.exp(-jnp.abs(fz))
            log_sig = jnp.minimum(fz, 0.0) - jnp.log1p(e)
            b = l1 + log_sig
            log_f = jnp.maximum(la, b) + jnp.log1p(jnp.exp(-jnp.abs(la - b)))
            rcp = 1.0 / (1.0 + e)
            k_b[pl.ds(r * ROW_CHUNK, ROW_CHUNK), :] = om * jnp.where(fz >= 0, e * rcp, rcp)
            parts = []
            for g in range(ROW_CHUNK // SUBLANES):
                cum = cum + log_f[g * SUBLANES:(g + 1) * SUBLANES, :]
                parts.append(cum)
            cum_b[pl.ds(r * ROW_CHUNK, ROW_CHUNK), :] = jnp.concatenate(parts, axis=0)
        cum_last = cum

        incl = cum_last
        for sh in (1, 2, 4):
            incl = incl + jnp.where(sub8 >= sh, pltpu.roll(incl, sh, axis=0), 0.0)
        offset = incl - cum_last
        total = _sublane_row(incl, SUBLANES - 1)
        totals[st, hd] = total

        seg_levels = []
        for c in range(TIME_BITS - SEG_BITS):
            upper8 = ((sub8 >> c) & 1) == 1
            incl_mid = jnp.zeros_like(incl)
            for s_mid in sorted({((s >> (c + 1)) << (c + 1)) | ((1 << c) - 1) for s in range(SUBLANES)}):
                in_block = (sub8 >> (c + 1)) == (s_mid >> (c + 1))
                incl_mid = jnp.where(in_block, _sublane_row(incl, s_mid), incl_mid)
            const = jnp.where(upper8, offset - incl_mid, incl_mid - incl + cum_last)
            seg_levels.append((((sub16 >> c) & 1) == 1, twice(const)))
        q_const = twice(offset)
        k_const = twice(cum_last + total - incl)

        for m in range(SEG // 2):
            rows = pl.ds(m * PAIR_ROWS, PAIR_ROWS)
            q = z_ref[pl.ds(base + m * PAIR_ROWS, PAIR_ROWS), q_l]
            k = k_b[rows, :]
            cum = cum_b[rows, :]
            for b in range(SEG_BITS):
                if b == 0:
                    upper = row16 >= SUBLANES
                    dec = jnp.exp(cum - twice(cum[0:SUBLANES, :]))
                    lvq[0, rows, :] = jnp.where(upper, q * dec, 0.0).astype(BF16)
                    lvk[0, rows, :] = jnp.where(upper, 0.0, k).astype(BF16)
                    continue
                mid = ((m >> b) << (b + 1)) | ((1 << b) - 1)
                ref = twice(cum_b[pl.ds(mid * SUBLANES, SUBLANES), :])
                if (m >> (b - 1)) & 1:
                    lvq[b, rows, :] = (q * jnp.exp(cum - ref)).astype(BF16)
                else:
                    lvk[b, rows, :] = (k * jnp.exp(ref - cum)).astype(BF16)
            ncum = -cum
            for b in range(SEG_BITS, TIME_BITS):
                upper, const = seg_levels[b - SEG_BITS]
                wgt = jnp.where(upper, q, k) * jnp.exp(jnp.where(upper, cum, ncum) + const)
                lvq[b, rows, :] = jnp.where(upper, wgt, 0.0).astype(BF16)
                lvk[b, rows, :] = jnp.where(upper, 0.0, wgt).astype(BF16)
            qh_b[rows, :] = (q * jnp.exp(cum + q_const)).astype(BF16)
            kh_b[rows, :] = (k * jnp.exp(ncum + k_const)).astype(BF16)
        vb_b[...] = z_ref[blk_rows(st), v_l].astype(BF16)

    def head_products(st, hd):
        par = hd % 2
        k_b, lvq, lvk = k_ref.at[par], lvq_ref.at[par], lvk_ref.at[par]
        qh_b, kh_b, vb_b, att_b = qh_ref.at[par], kh_ref.at[par], vb_ref.at[par], att_ref.at[par]
        q_l = pl.ds(q_c + hd * LANES, LANES)
        v_l = pl.ds(v_c + hd * LANES, LANES)
        for b in range(TIME_BITS):
            part = lax.dot_general(lvq[b], lvk[b], nt, preferred_element_type=F32)
            hit = lvl_ref[...] == float(b)
            att_b[...] = jnp.where(hit, part, 0.0 if b == 0 else att_b[...])

        state = state_ref[hd]
        diag = jnp.sum(z_ref[blk_rows(st), q_l] * k_b[...], axis=-1, keepdims=True)
        o = jnp.dot(att_b[...].astype(BF16), vb_b[...], preferred_element_type=F32)
        o = o + lax.dot_general(qh_b[...], state.astype(BF16), nt, preferred_element_type=F32)
        oh_ref[hd, blk_rows(st), :] = o + diag * z_ref[blk_rows(st), v_l]
        upd = lax.dot_general(vb_b[...], kh_b[...], tn, preferred_element_type=F32)
        state_ref[hd] = state * jnp.exp(totals[st, hd][0:1, :]) + upd

    def head_outputs(st, r):
        rows = pl.ds(st * PT + r * ROW_CHUNK, ROW_CHUNK)
        for hd in range(heads):
            o = oh_ref[hd, rows, :]
            o = o * lax.rsqrt(jnp.mean(o * o, axis=-1, keepdims=True) + EPS) * ng_ref[:, pl.ds(hd * LANES, LANES)]
            gd = z_ref[rows, pl.ds(gd_c + hd * LANES, LANES)]
            y_ref[rows, pl.ds(w + hd * LANES, LANES)] = (o * (gd * _sigmoid(gd))).astype(BF16)

    n_out = x_ref.shape[1] // MXU_DIM
    _rmsnorm_to(x_ref, g_ref, h_ref, 0, PT)
    for k in range(5):
        project(0, k)
    for st in range(NSUB):
        pieces = []
        if st + 1 < NSUB:
            pieces += [functools.partial(project, st + 1, k) for k in range(5)]
        if st > 0:
            pieces += [functools.partial(out_project, st - 1, n) for n in range(n_out)]
        steps = [functools.partial(pool, st)]
        for hd in range(heads):
            steps += [functools.partial(head_factors, st, hd), functools.partial(head_products, st, hd)]
        steps += [functools.partial(head_outputs, st, r) for r in range(CHUNKS_PER_BLOCK)]
        if st + 1 < NSUB:
            _rmsnorm_to(x_ref, g_ref, h_ref, (st + 1) * PT, PT)
        emitted = 0
        for idx, step in enumerate(steps):
            step()
            due = (idx + 1) * len(pieces) // len(steps)
            while emitted < due:
                pieces[emitted]()
                emitted += 1
    for n in range(n_out):
        out_project(NSUB - 1, n)


def _odd_call(x, params, layer):
    bsz, seq, d = x.shape
    g, win, pw, pscale, lbl, ng, levels, wout = params
    zin = win.shape[-1]
    w = zin // 5
    heads = w // LANES
    hist = POOL_WINDOWS[-1] // 2
    o = layer // 2
    xspec = pl.BlockSpec((None, SEQ_TILE, d), lambda b, j: (b, j, 0))
    specs = [_layer_spec(g, layer), _layer_spec(win, o), _layer_spec(pw, o), _layer_spec(pscale, o),
             _whole_spec(lbl), _layer_spec(ng, o), _whole_spec(levels), _layer_spec(wout, o)]
    return pl.pallas_call(
        functools.partial(_odd_kernel, layer=o),
        out_shape=jax.ShapeDtypeStruct(x.shape, x.dtype),
        grid=(bsz, seq // SEQ_TILE),
        in_specs=[xspec] + specs,
        out_specs=xspec,
        scratch_shapes=[
            pltpu.VMEM((SEQ_TILE, d), BF16),
            pltpu.VMEM((SEQ_TILE, zin), F32),
            pltpu.VMEM((PT + hist * SUBLANES, w), F32),
            pltpu.VMEM((PT + hist * SUBLANES, w), F32),
            pltpu.VMEM(((POOL_WINDOWS[-1] - 1) * SUBLANES, w), F32),
            pltpu.VMEM((2, PT, LANES), F32),
            pltpu.VMEM((2, PT, LANES), F32),
            pltpu.VMEM((2, TIME_BITS, PT, LANES), BF16),
            pltpu.VMEM((2, TIME_BITS, PT, LANES), BF16),
            pltpu.VMEM((2, PT, LANES), BF16),
            pltpu.VMEM((2, PT, LANES), BF16),
            pltpu.VMEM((2, PT, LANES), BF16),
            pltpu.VMEM((2, PT, PT), F32),
            pltpu.VMEM((heads, SEQ_TILE, LANES), F32),
            pltpu.VMEM((SEQ_TILE, 2 * w), BF16),
            pltpu.VMEM((heads, LANES, LANES), F32),
        ],
        compiler_params=_cparams(),
        name="pool_hgrn_mixer",
    )(x, *params)


def _cast_kernel(x_ref, o_ref):
    o_ref[...] = x_ref[...].astype(o_ref.dtype)


def _to_bf16(wstack):
    rows = int(np.prod(wstack.shape[:-1]))
    cols = wstack.shape[-1]
    assert rows % CAST_ROWS == 0
    spec = pl.BlockSpec((CAST_ROWS, cols), lambda i: (i, 0))
    out = pl.pallas_call(
        _cast_kernel,
        out_shape=jax.ShapeDtypeStruct((rows, cols), BF16),
        grid=(rows // CAST_ROWS,),
        in_specs=[spec],
        out_specs=spec,
        name="weights_to_bf16",
    )(wstack.reshape(rows, cols))
    return out.reshape(wstack.shape)


def _to_segment_major(x):
    b, s, d = x.shape
    return x.reshape(b, s // PT, SUBLANES, SEG, d).transpose(0, 1, 3, 2, 4).reshape(b, s, d)


def _from_segment_major(x):
    b, s, d = x.shape
    return x.reshape(b, s // PT, SEG, SUBLANES, d).transpose(0, 1, 3, 2, 4).reshape(b, s, d)


def _block_diag_tiles(wblocks):
    layers, nb, n, _ = wblocks.shape
    per = MXU_DIM // n
    grouped = wblocks.reshape(layers, nb // per, per, n, n)
    tiles = jnp.einsum("ab,ltaij->ltaibj", jnp.eye(per, dtype=wblocks.dtype), grouped)
    return tiles.reshape(layers, nb // per, MXU_DIM, MXU_DIM).astype(BF16)


def kernel(x, g_mix, g_ffn, g_final, w_in_even, w_out_even, lru_conv_w, lru_conv_b, lru_wa, lru_ba, lru_wi, lru_bi, lru_lambda, sconv_w, w_in_odd, w_out_odd, pool_w, pool_scale, hgrn_lb_logits, hgrn_norm_g, ffn_w_up, ffn_w_gate, ffn_conv_w, ffn_conv_b, ffn_w_down):
    depth = g_mix.shape[0]
    assert x.shape[1] % SEQ_TILE == 0
    rows = lambda v: v.reshape(v.shape[0], 1, -1).astype(F32)
    levels = jnp.asarray(_pair_levels())
    even = (rows(g_mix), _to_bf16(w_in_even), lru_conv_w, rows(lru_conv_b), _block_diag_tiles(lru_wa),
            _block_diag_tiles(lru_wi), rows(lru_ba), rows(lru_bi), rows(lru_lambda), sconv_w, _to_bf16(w_out_even))
    odd = (rows(g_mix), _to_bf16(w_in_odd), _block_diag_tiles(pool_w), rows(pool_scale),
           hgrn_lb_logits.astype(F32), rows(hgrn_norm_g), levels, _to_bf16(w_out_odd))
    ffn = (rows(g_ffn), _to_bf16(ffn_w_up), _to_bf16(ffn_w_gate), ffn_conv_w, rows(ffn_conv_b),
           _to_bf16(ffn_w_down), g_final.reshape(1, -1).astype(F32))
    x = _to_segment_major(x)
    for l in range(depth):
        if l % 2 == 0:
            x = _even_call(x, even, l)
        else:
            x = _odd_call(x, odd, l)
        x = _ffn_call(x, ffn, l, l == depth - 1)
    return _from_segment_major(x)
```

```python
import functools

import jax
import jax.numpy as jnp
import numpy as np
from jax import lax
from jax.experimental import pallas as pl
from jax.experimental.pallas import tpu as pltpu

F32 = jnp.float32
BF16 = jnp.bfloat16

SUBLANES = 8
LANES = 128
MXU_DIM = 256
VMEM_LIMIT_BYTES = 56 * 1024 * 1024

EPS = 1e-6
LOG2_E = 1.4426950408889634
LRU_C = 8.0
POOL_WINDOWS = (2, 4, 8, 16)

SEQ_TILE = 512
PT = 128
NSUB = SEQ_TILE // PT
SEG = PT // SUBLANES
SEG_BITS = SEG.bit_length() - 1
TIME_BITS = PT.bit_length() - 1
ROW_CHUNK = 64
CHUNKS_PER_BLOCK = PT // ROW_CHUNK
FFN_CHUNK = 256
PAIR_ROWS = 2 * SUBLANES
CAST_ROWS = 512


def _cparams():
    return pltpu.CompilerParams(
        dimension_semantics=("arbitrary", "arbitrary"),
        vmem_limit_bytes=VMEM_LIMIT_BYTES,
    )


def _sigmoid(v):
    return 0.5 * jnp.tanh(0.5 * v) + 0.5


def _rmsnorm_rows(x, g):
    ms = jnp.mean(x * x, axis=-1, keepdims=True)
    return x * lax.rsqrt(ms + EPS) * g


def _rmsnorm_to(x_ref, g_ref, h_ref, row0=0, nrows=SEQ_TILE):
    for r in range(nrows // ROW_CHUNK):
        rows = pl.ds(row0 + r * ROW_CHUNK, ROW_CHUNK)
        h_ref[rows, :] = _rmsnorm_rows(x_ref[rows, :], g_ref[...]).astype(h_ref.dtype)


def _fill_history(buf_ref, base, carry_ref, groups, lanes, carry_lanes=None):
    carry_lanes = lanes if carry_lanes is None else carry_lanes
    for g in range(groups):
        tail = buf_ref[pl.ds(base + PT + g * SUBLANES, SUBLANES), lanes]
        rolled = pltpu.roll(tail, 1, axis=0)
        sub = lax.broadcasted_iota(jnp.int32, rolled.shape, 0)
        prev = carry_ref[pl.ds(g * SUBLANES, SUBLANES), carry_lanes]
        buf_ref[pl.ds(base + g * SUBLANES, SUBLANES), lanes] = jnp.where(sub == 0, prev, rolled)
        carry_ref[pl.ds(g * SUBLANES, SUBLANES), carry_lanes] = rolled


def _row_time(t0, shape):
    row = lax.broadcasted_iota(jnp.int32, shape, 0)
    sub = row & (SUBLANES - 1)
    grp = row >> (SUBLANES.bit_length() - 1)
    return t0 + sub * SEG + grp


def _layer_spec(a, l):
    return pl.BlockSpec((None,) + a.shape[1:], lambda b, j: (l,) + (0,) * (a.ndim - 1))


def _whole_spec(a):
    return pl.BlockSpec(a.shape, lambda b, j: (0,) * a.ndim)


def _ffn_kernel(x_ref, g_ref, wup_ref, wgate_ref, cw_ref, cb_ref, wdown_ref, gfin_ref, o_ref,
                h_ref, up_ref, gate_ref, act_ref, carry_ref, *, final_norm):
    hist = 2
    hrows = hist * SUBLANES
    stride = hrows + PT
    ffn_dim = wup_ref.shape[1]

    @pl.when(pl.program_id(1) == 0)
    def _():
        carry_ref[...] = jnp.zeros_like(carry_ref)

    _rmsnorm_to(x_ref, g_ref, h_ref)

    for c in range(ffn_dim // FFN_CHUNK):
        cols = pl.ds(c * FFN_CHUNK, FFN_CHUNK)
        up = up_ref.at[c % 2]
        gate = gate_ref.at[c % 2]
        proj = jnp.dot(h_ref[...], wup_ref[:, cols], preferred_element_type=F32)
        for st in range(NSUB):
            up[pl.ds(st * stride + hrows, PT), :] = proj[st * PT:(st + 1) * PT, :]
        gate[...] = jnp.dot(h_ref[...], wgate_ref[:, cols], preferred_element_type=F32)
        w0 = cw_ref[0:1, cols]
        w1 = cw_ref[1:2, cols]
        w2 = cw_ref[2:3, cols]
        b = cb_ref[:, cols]
        for st in range(NSUB):
            _fill_history(up, st * stride, carry_ref, hist, slice(None), cols)
            for r in range(CHUNKS_PER_BLOCK):
                r0 = st * stride + r * ROW_CHUNK
                u = (up[pl.ds(r0 + hrows, ROW_CHUNK), :] * w2
                     + up[pl.ds(r0 + hrows - SUBLANES, ROW_CHUNK), :] * w1
                     + up[pl.ds(r0, ROW_CHUNK), :] * w0 + b)
                rows = pl.ds(st * PT + r * ROW_CHUNK, ROW_CHUNK)
                act_ref[rows, cols] = (jax.nn.gelu(u) * gate[rows, :]).astype(BF16)

    o_ref[...] = x_ref[...] + jnp.dot(act_ref[...], wdown_ref[...], preferred_element_type=F32)
    if final_norm:
        for r in range(SEQ_TILE // ROW_CHUNK):
            rows = pl.ds(r * ROW_CHUNK, ROW_CHUNK)
            o_ref[rows, :] = _rmsnorm_rows(o_ref[rows, :], gfin_ref[...])


def _ffn_call(x, params, layer, final_norm):
    bsz, seq, d = x.shape
    g, wup, wgate, cw, cb, wdown, gfin = params
    f = wup.shape[-1]
    xspec = pl.BlockSpec((None, SEQ_TILE, d), lambda b, j: (b, j, 0))
    return pl.pallas_call(
        functools.partial(_ffn_kernel, final_norm=final_norm),
        out_shape=jax.ShapeDtypeStruct(x.shape, x.dtype),
        grid=(bsz, seq // SEQ_TILE),
        in_specs=[xspec] + [_layer_spec(a, layer) for a in params[:-1]] + [_whole_spec(gfin)],
        out_specs=xspec,
        scratch_shapes=[
            pltpu.VMEM((SEQ_TILE, d), BF16),
            pltpu.VMEM((2, NSUB * (PT + 2 * SUBLANES), FFN_CHUNK), F32),
            pltpu.VMEM((2, SEQ_TILE, FFN_CHUNK), F32),
            pltpu.VMEM((SEQ_TILE, f), BF16),
            pltpu.VMEM((2 * SUBLANES, f), F32),
        ],
        compiler_params=_cparams(),
        name="conv_ffn",
    )(x, *params)


def _even_kernel(x_ref, g_ref, win_ref, cw_ref, cb_ref, wa_ref, wi_ref, ba_ref, bi_ref, lam_ref,
                 sw_ref, wout_ref, o_ref,
                 h_ref, *scratch):
    z_refs = scratch[:NSUB]
    xc_ref, xcb_ref, gate_ref, y_ref, zcarry_ref, hcarry_ref = scratch[NSUB:]
    hist = 3
    hrows = hist * SUBLANES
    w = xc_ref.shape[1]
    j = pl.program_id(1)

    @pl.when(j == 0)
    def _():
        zcarry_ref[...] = jnp.zeros_like(zcarry_ref)
        hcarry_ref[...] = jnp.zeros_like(hcarry_ref)

    xa_c, ga_c, hb_c, bg_c, cg_c = (pl.ds(k * w, w) for k in range(5))
    r_l, i_l = pl.ds(0, w), pl.ds(w, w)
    sub = lax.broadcasted_iota(jnp.int32, (SUBLANES, w), 0)

    lam = lam_ref[...]
    neg_c_sp = (-LRU_C * LOG2_E) * (jnp.maximum(-lam, 0.0) + jnp.log1p(jnp.exp(-jnp.abs(lam))))

    blk_rows = lambda st: pl.ds(st * PT, PT)
    chunk_rows = lambda st, r: pl.ds(st * PT + r * ROW_CHUNK, ROW_CHUNK)
    h_rep = {}

    def project(st, k):
        cols = pl.ds(k * w, w)
        z_refs[st][pl.ds(hrows, PT), cols] = jnp.dot(
            h_ref[blk_rows(st), :], win_ref[:, cols], preferred_element_type=F32)

    def history(st):
        for lanes in (xa_c, hb_c, cg_c):
            _fill_history(z_refs[st], 0, zcarry_ref, hist, lanes)

    def lru_conv(st, r):
        z_ref = z_refs[st]
        z0 = hrows + r * ROW_CHUNK
        acc = cb_ref[...] + z_ref[pl.ds(z0, ROW_CHUNK), xa_c] * cw_ref[3:4, :]
        for k in range(1, 4):
            acc = acc + z_ref[pl.ds(z0 - k * SUBLANES, ROW_CHUNK), xa_c] * cw_ref[3 - k:4 - k, :]
        xc_ref[chunk_rows(st, r), :] = acc
        xcb_ref[chunk_rows(st, r), :] = acc.astype(BF16)

    def gate_matmuls(st):
        for blk in range(w // MXU_DIM):
            lanes = pl.ds(blk * MXU_DIM, MXU_DIM)
            gate_ref[blk_rows(st), pl.ds(blk * MXU_DIM, MXU_DIM)] = jnp.dot(
                xcb_ref[blk_rows(st), lanes], wa_ref[blk], preferred_element_type=F32)
            gate_ref[blk_rows(st), pl.ds(w + blk * MXU_DIM, MXU_DIM)] = jnp.dot(
                xcb_ref[blk_rows(st), lanes], wi_ref[blk], preferred_element_type=F32)

    def gates(st, r):
        rows = chunk_rows(st, r)
        rg = _sigmoid(gate_ref[rows, r_l] + ba_ref[...])
        ig = _sigmoid(gate_ref[rows, i_l] + bi_ref[...])
        a = jnp.exp2(rg * neg_c_sp)
        mult = jnp.sqrt(1.0 - a * a)
        t = _row_time(j * SEQ_TILE + st * PT + r * (ROW_CHUNK // SUBLANES), mult.shape)
        mult = jnp.where(t == 0, 1.0, mult)
        xc_ref[rows, :] = mult * ig * xc_ref[rows, :]
        gate_ref[rows, r_l] = a

    def scan(st):
        h = jnp.zeros((SUBLANES, w), F32)
        p = h + 1.0
        for i in range(SEG):
            rows = pl.ds(st * PT + i * SUBLANES, SUBLANES)
            a = gate_ref[rows, r_l]
            h = a * h + xc_ref[rows, :]
            p = a * p
            xc_ref[rows, :] = h
            gate_ref[rows, r_l] = p
        h_in = jnp.where(sub == 0, hcarry_ref[...], 0.0)
        for s in range(SUBLANES - 1):
            nxt = pltpu.roll(h + p * h_in, 1, axis=0)
            h_in = jnp.where(sub == s + 1, nxt, h_in)
        hcarry_ref[...] = pltpu.roll(h + p * h_in, 1, axis=0)
        h_rep[st] = jnp.concatenate([h_in] * (ROW_CHUNK // SUBLANES), axis=0)

    def outputs(st, r):
        z_ref = z_refs[st]
        rows = chunk_rows(st, r)
        z0 = hrows + r * ROW_CHUNK
        hs = xc_ref[rows, :] + gate_ref[rows, r_l] * h_rep[st]
        ya = hs * jax.nn.gelu(z_ref[pl.ds(z0, ROW_CHUNK), ga_c])
        conv = None
        for k in range(3):
            zrows = pl.ds(z0 - k * SUBLANES, ROW_CHUNK)
            term = z_ref[zrows, cg_c] * z_ref[zrows, hb_c] * sw_ref[2 - k:3 - k, :]
            conv = term if conv is None else conv + term
        yb = z_ref[pl.ds(z0, ROW_CHUNK), bg_c] * conv
        y_ref[rows, pl.ds(0, w)] = ya.astype(BF16)
        y_ref[rows, pl.ds(w, w)] = yb.astype(BF16)

    def out_project(st, n):
        cols = pl.ds(n * MXU_DIM, MXU_DIM)
        o_ref[blk_rows(st), cols] = x_ref[blk_rows(st), cols] + jnp.dot(
            y_ref[blk_rows(st), :], wout_ref[:, cols], preferred_element_type=F32)

    d_model = x_ref.shape[1]
    n_out = d_model // MXU_DIM
    _rmsnorm_to(x_ref, g_ref, h_ref, 0, PT)
    for k in range(5):
        project(0, k)
    for st in range(NSUB):
        nxt = st + 1 if st + 1 < NSUB else None
        prv = st - 1 if st > 0 else None
        pieces = []
        if nxt is not None:
            pieces += [functools.partial(project, nxt, k) for k in range(5)]
        if prv is not None:
            pieces += [functools.partial(out_project, prv, n) for n in range(n_out)]
        vector_steps = ([functools.partial(lru_conv, st, r) for r in range(CHUNKS_PER_BLOCK)]
                        + [functools.partial(gate_matmuls, st)]
                        + [functools.partial(gates, st, r) for r in range(CHUNKS_PER_BLOCK)]
                        + [functools.partial(scan, st)]
                        + [functools.partial(outputs, st, r) for r in range(CHUNKS_PER_BLOCK)])
        history(st)
        if nxt is not None:
            _rmsnorm_to(x_ref, g_ref, h_ref, nxt * PT, PT)
        emitted = 0
        for idx, step in enumerate(vector_steps):
            step()
            due = (idx + 1) * len(pieces) // len(vector_steps)
            while emitted < due:
                pieces[emitted]()
                emitted += 1
    for n in range(n_out):
        out_project(NSUB - 1, n)


def _even_call(x, params, layer):
    bsz, seq, d = x.shape
    zin = params[1].shape[-1]
    w = zin // 5
    xspec = pl.BlockSpec((None, SEQ_TILE, d), lambda b, j: (b, j, 0))
    specs = [_layer_spec(params[0], layer)] + [_layer_spec(a, layer // 2) for a in params[1:]]
    return pl.pallas_call(
        _even_kernel,
        out_shape=jax.ShapeDtypeStruct(x.shape, x.dtype),
        grid=(bsz, seq // SEQ_TILE),
        in_specs=[xspec] + specs,
        out_specs=xspec,
        scratch_shapes=[
            pltpu.VMEM((SEQ_TILE, d), BF16),
        ] + [pltpu.VMEM((PT + 3 * SUBLANES, zin), F32) for _ in range(NSUB)] + [
            pltpu.VMEM((SEQ_TILE, w), F32),
            pltpu.VMEM((SEQ_TILE, w), BF16),
            pltpu.VMEM((SEQ_TILE, 2 * w), F32),
            pltpu.VMEM((SEQ_TILE, 2 * w), BF16),
            pltpu.VMEM((3 * SUBLANES, zin), F32),
            pltpu.VMEM((SUBLANES, w), F32),
        ],
        compiler_params=_cparams(),
        name="lru_sconv_mixer",
    )(x, *params)


def _pair_levels():
    p = np.arange(PT)
    t = (p % SUBLANES) * SEG + p // SUBLANES
    x = t[:, None] ^ t[None, :]
    lvl = np.full(x.shape, -1.0, np.float32)
    for b in range(TIME_BITS):
        lvl[(x >> b) == 1] = b
    return lvl


def _sublane_row(v, s):
    return jnp.broadcast_to(v[s:s + 1, :], v.shape)


def _odd_kernel(x_ref, g_ref, win_ref, pw_ref, pscale_ref, lbl_ref, ng_ref, lvl_ref, wout_ref, o_ref,
                h_ref, z_ref, pa_ref, pb_ref, pcarry_ref, cum_ref, k_ref, lvq_ref, lvk_ref, qh_ref, kh_ref,
                vb_ref, oh_ref, y_ref, state_ref, *, layer):
    w = pscale_ref.shape[1]
    heads = w // LANES
    j = pl.program_id(1)
    hist = POOL_WINDOWS[-1] // 2
    hrows = hist * SUBLANES

    @pl.when(jnp.logical_and(pl.program_id(0) == 0, j == 0))
    def _():
        lvq_ref[...] = jnp.zeros_like(lvq_ref)
        lvk_ref[...] = jnp.zeros_like(lvk_ref)

    @pl.when(j == 0)
    def _():
        pcarry_ref[...] = jnp.zeros_like(pcarry_ref)
        state_ref[...] = jnp.zeros_like(state_ref)

    uc_c, q_c, fz_c, v_c, gd_c = (k * w for k in range(5))

    logits = lbl_ref[...]
    sm = jnp.exp(logits - jnp.max(logits, axis=0, keepdims=True))
    sm = sm / jnp.sum(sm, axis=0, keepdims=True)
    lb = jnp.zeros((1, w), F32)
    for o in range(1, layer + 1):
        lb = lb + sm[o:o + 1, :]
    log_lb = jnp.log(lb)
    log_1m_lb = jnp.log1p(-lb)

    sub8 = lax.broadcasted_iota(jnp.int32, (SUBLANES, LANES), 0)
    row16 = lax.broadcasted_iota(jnp.int32, (PAIR_ROWS, LANES), 0)
    sub16 = row16 & (SUBLANES - 1)
    nt = (((1,), (1,)), ((), ()))
    tn = (((0,), (0,)), ((), ()))
    twice = lambda v: jnp.concatenate([v, v], axis=0)
    bufs = (pa_ref, pb_ref)

    blk_rows = lambda st: pl.ds(st * PT, PT)

    def pool(st):
        base = st * PT
        pa_ref[pl.ds(hrows, PT), :] = z_ref[blk_rows(st), pl.ds(uc_c, w)]
        for stage in range(len(POOL_WINDOWS)):
            shift = 1 << stage
            src = bufs[stage % 2]
            dst = bufs[(stage + 1) % 2]
            lanes = pl.ds(stage * LANES, w - stage * LANES)
            for gidx in range(shift):
                tail = src[pl.ds(hrows + PT - (shift - gidx) * SUBLANES, SUBLANES), lanes]
                rolled = pltpu.roll(tail, 1, axis=0)
                crow = pl.ds((shift - 1 + gidx) * SUBLANES, SUBLANES)
                prev = pcarry_ref[crow, lanes]
                src[pl.ds(hrows - (shift - gidx) * SUBLANES, SUBLANES), lanes] = jnp.where(
                    lax.broadcasted_iota(jnp.int32, rolled.shape, 0) == 0, prev, rolled)
                pcarry_ref[crow, lanes] = rolled
            for r in range(CHUNKS_PER_BLOCK):
                r0 = r * ROW_CHUNK + hrows
                dst[pl.ds(r0, ROW_CHUNK), lanes] = (
                    src[pl.ds(r0, ROW_CHUNK), lanes] + src[pl.ds(r0 - shift * SUBLANES, ROW_CHUNK), lanes])

        for r in range(CHUNKS_PER_BLOCK):
            t = _row_time(j * SEQ_TILE + base + r * (ROW_CHUNK // SUBLANES), (ROW_CHUNK, LANES))
            for gi, win in enumerate(POOL_WINDOWS):
                buf = bufs[(gi + 1) % 2]
                lanes = pl.ds(gi * LANES, LANES)
                cnt = jnp.minimum(t + 1, win).astype(F32)
                rows = pl.ds(base + r * ROW_CHUNK, ROW_CHUNK)
                p = buf[pl.ds(hrows + r * ROW_CHUNK, ROW_CHUNK), lanes] / cnt - z_ref[rows, pl.ds(uc_c + gi * LANES, LANES)]
                y_ref[rows, lanes] = p.astype(BF16)

        for blk in range(w // MXU_DIM):
            lanes = pl.ds(blk * MXU_DIM, MXU_DIM)
            mixed = jnp.dot(y_ref[blk_rows(st), lanes], pw_ref[blk], preferred_element_type=F32)
            y_ref[blk_rows(st), lanes] = (mixed * pscale_ref[:, lanes]).astype(BF16)

    totals = {}

    def head_factors(st, hd):
        base = st * PT
        par = hd % 2
        cum_b, k_b, lvq, lvk = cum_ref.at[par], k_ref.at[par], lvq_ref.at[par], lvk_ref.at[par]
        qh_b, kh_b, vb_b = qh_ref.at[par], kh_ref.at[par], vb_ref.at[par]
        la = log_lb[:, hd * LANES:(hd + 1) * LANES]
        l1 = log_1m_lb[:, hd * LANES:(hd + 1) * LANES]
        om = 1.0 - lb[:, hd * LANES:(hd + 1) * LANES]
        q_l = pl.ds(q_c + hd * LANES, LANES)
        fz_l = pl.ds(fz_c + hd * LANES, LANES)
        v_l = pl.ds(v_c + hd * LANES, LANES)

        cum = jnp.zeros((SUBLANES, LANES), F32)
        for r in range(CHUNKS_PER_BLOCK):
            fz = z_ref[pl.ds(base + r * ROW_CHUNK, ROW_CHUNK), fz_l]
            e = jnp.exp(-jnp.abs(fz))
            log_sig = jnp.minimum(fz, 0.0) - jnp.log1p(e)
            b = l1 + log_sig
            log_f = LOG2_E * (jnp.maximum(la, b) + jnp.log1p(jnp.exp(-jnp.abs(la - b))))
            rcp = 1.0 / (1.0 + e)
            k_b[pl.ds(r * ROW_CHUNK, ROW_CHUNK), :] = om * jnp.where(fz >= 0, e * rcp, rcp)
            parts = []
            for g in range(ROW_CHUNK // SUBLANES):
                cum = cum + log_f[g * SUBLANES:(g + 1) * SUBLANES, :]
                parts.append(cum)
            cum_b[pl.ds(r * ROW_CHUNK, ROW_CHUNK), :] = jnp.concatenate(parts, axis=0)
        cum_last = cum

        incl = cum_last
        for sh in (1, 2, 4):
            incl = incl + jnp.where(sub8 >= sh, pltpu.roll(incl, sh, axis=0), 0.0)
        offset = incl - cum_last
        total = _sublane_row(incl, SUBLANES - 1)
        totals[st, hd] = total

        seg_levels = []
        for c in range(TIME_BITS - SEG_BITS):
            upper8 = ((sub8 >> c) & 1) == 1
            incl_mid = jnp.zeros_like(incl)
            for s_mid in sorted({((s >> (c + 1)) << (c + 1)) | ((1 << c) - 1) for s in range(SUBLANES)}):
                in_block = (sub8 >> (c + 1)) == (s_mid >> (c + 1))
                incl_mid = jnp.where(in_block, _sublane_row(incl, s_mid), incl_mid)
            const = jnp.where(upper8, offset - incl_mid, incl_mid - incl + cum_last)
            seg_levels.append((((sub16 >> c) & 1) == 1, twice(const)))
        q_const = twice(offset)
        k_const = twice(cum_last + total - incl)

        for m in range(SEG // 2):
            rows = pl.ds(m * PAIR_ROWS, PAIR_ROWS)
            q = z_ref[pl.ds(base + m * PAIR_ROWS, PAIR_ROWS), q_l]
            k = k_b[rows, :]
            cum = cum_b[rows, :]
            for b in range(SEG_BITS):
                if b == 0:
                    upper = row16 >= SUBLANES
                    dec = jnp.exp2(cum - twice(cum[0:SUBLANES, :]))
                    lvq[0, rows, :] = jnp.where(upper, q * dec, 0.0).astype(BF16)
                    lvk[0, rows, :] = jnp.where(upper, 0.0, k).astype(BF16)
                    continue
                mid = ((m >> b) << (b + 1)) | ((1 << b) - 1)
                ref = twice(cum_b[pl.ds(mid * SUBLANES, SUBLANES), :])
                if (m >> (b - 1)) & 1:
                    lvq[b, rows, :] = (q * jnp.exp2(cum - ref)).astype(BF16)
                else:
                    lvk[b, rows, :] = (k * jnp.exp2(ref - cum)).astype(BF16)
            ncum = -cum
            for b in range(SEG_BITS, TIME_BITS):
                upper, const = seg_levels[b - SEG_BITS]
                wgt = jnp.where(upper, q, k) * jnp.exp2(jnp.where(upper, cum, ncum) + const)
                lvq[b, rows, :] = jnp.where(upper, wgt, 0.0).astype(BF16)
                lvk[b, rows, :] = jnp.where(upper, 0.0, wgt).astype(BF16)
            qh_b[rows, :] = (q * jnp.exp2(cum + q_const)).astype(BF16)
            kh_b[rows, :] = (k * jnp.exp2(ncum + k_const)).astype(BF16)
        vb_b[...] = z_ref[blk_rows(st), v_l].astype(BF16)

    def head_products(st, hd):
        par = hd % 2
        k_b, lvq, lvk = k_ref.at[par], lvq_ref.at[par], lvk_ref.at[par]
        qh_b, kh_b, vb_b = qh_ref.at[par], kh_ref.at[par], vb_ref.at[par]
        q_l = pl.ds(q_c + hd * LANES, LANES)
        v_l = pl.ds(v_c + hd * LANES, LANES)
        att = jnp.zeros((PT, PT), F32)
        for b in range(TIME_BITS):
            part = lax.dot_general(lvq[b], lvk[b], nt, preferred_element_type=F32)
            att = jnp.where(lvl_ref[...] == float(b), part, att)

        state = state_ref[hd]
        diag = jnp.sum(z_ref[blk_rows(st), q_l] * k_b[...], axis=-1, keepdims=True)
        o = jnp.dot(att.astype(BF16), vb_b[...], preferred_element_type=F32)
        o = o + lax.dot_general(qh_b[...], state.astype(BF16), nt, preferred_element_type=F32)
        oh_ref[hd, blk_rows(st), :] = o + diag * z_ref[blk_rows(st), v_l]
        upd = lax.dot_general(vb_b[...], kh_b[...], tn, preferred_element_type=F32)
        state_ref[hd] = state * jnp.exp2(totals[st, hd][0:1, :]) + upd

    def head_outputs(st, r):
        rows = pl.ds(st * PT + r * ROW_CHUNK, ROW_CHUNK)
        for hd in range(heads):
            o = oh_ref[hd, rows, :]
            o = o * lax.rsqrt(jnp.mean(o * o, axis=-1, keepdims=True) + EPS) * ng_ref[:, pl.ds(hd * LANES, LANES)]
            gd = z_ref[rows, pl.ds(gd_c + hd * LANES, LANES)]
            y_ref[rows, pl.ds(w + hd * LANES, LANES)] = (o * (gd * _sigmoid(gd))).astype(BF16)

    _rmsnorm_to(x_ref, g_ref, h_ref)
    z_ref[...] = jnp.dot(h_ref[...], win_ref[...], preferred_element_type=F32)
    for st in range(NSUB):
        pool(st)
        for hd in range(heads):
            head_factors(st, hd)
            head_products(st, hd)
        for r in range(CHUNKS_PER_BLOCK):
            head_outputs(st, r)
    o_ref[...] = x_ref[...] + jnp.dot(y_ref[...], wout_ref[...], preferred_element_type=F32)


def _odd_call(x, params, layer):
    bsz, seq, d = x.shape
    g, win, pw, pscale, lbl, ng, levels, wout = params
    zin = win.shape[-1]
    w = zin // 5
    heads = w // LANES
    hist = POOL_WINDOWS[-1] // 2
    o = layer // 2
    xspec = pl.BlockSpec((None, SEQ_TILE, d), lambda b, j: (b, j, 0))
    specs = [_layer_spec(g, layer), _layer_spec(win, o), _layer_spec(pw, o), _layer_spec(pscale, o),
             _whole_spec(lbl), _layer_spec(ng, o), _whole_spec(levels), _layer_spec(wout, o)]
    return pl.pallas_call(
        functools.partial(_odd_kernel, layer=o),
        out_shape=jax.ShapeDtypeStruct(x.shape, x.dtype),
        grid=(bsz, seq // SEQ_TILE),
        in_specs=[xspec] + specs,
        out_specs=xspec,
        scratch_shapes=[
            pltpu.VMEM((SEQ_TILE, d), BF16),
            pltpu.VMEM((SEQ_TILE, zin), F32),
            pltpu.VMEM((PT + hist * SUBLANES, w), F32),
            pltpu.VMEM((PT + hist * SUBLANES, w), F32),
            pltpu.VMEM(((POOL_WINDOWS[-1] - 1) * SUBLANES, w), F32),
            pltpu.VMEM((2, PT, LANES), F32),
            pltpu.VMEM((2, PT, LANES), F32),
            pltpu.VMEM((2, TIME_BITS, PT, LANES), BF16),
            pltpu.VMEM((2, TIME_BITS, PT, LANES), BF16),
            pltpu.VMEM((2, PT, LANES), BF16),
            pltpu.VMEM((2, PT, LANES), BF16),
            pltpu.VMEM((2, PT, LANES), BF16),
            pltpu.VMEM((heads, SEQ_TILE, LANES), F32),
            pltpu.VMEM((SEQ_TILE, 2 * w), BF16),
            pltpu.VMEM((heads, LANES, LANES), F32),
        ],
        compiler_params=_cparams(),
        name="pool_hgrn_mixer",
    )(x, *params)


def _cast_kernel(x_ref, o_ref):
    o_ref[...] = x_ref[...].astype(o_ref.dtype)


def _to_bf16(wstack):
    rows = int(np.prod(wstack.shape[:-1]))
    cols = wstack.shape[-1]
    assert rows % CAST_ROWS == 0
    spec = pl.BlockSpec((CAST_ROWS, cols), lambda i: (i, 0))
    out = pl.pallas_call(
        _cast_kernel,
        out_shape=jax.ShapeDtypeStruct((rows, cols), BF16),
        grid=(rows // CAST_ROWS,),
        in_specs=[spec],
        out_specs=spec,
        name="weights_to_bf16",
    )(wstack.reshape(rows, cols))
    return out.reshape(wstack.shape)


def _to_segment_major(x):
    b, s, d = x.shape
    return x.reshape(b, s // PT, SUBLANES, SEG, d).transpose(0, 1, 3, 2, 4).reshape(b, s, d)


def _from_segment_major(x):
    b, s, d = x.shape
    return x.reshape(b, s // PT, SEG, SUBLANES, d).transpose(0, 1, 3, 2, 4).reshape(b, s, d)


def _block_diag_tiles(wblocks):
    layers, nb, n, _ = wblocks.shape
    per = MXU_DIM // n
    grouped = wblocks.reshape(layers, nb // per, per, n, n)
    tiles = jnp.einsum("ab,ltaij->ltaibj", jnp.eye(per, dtype=wblocks.dtype), grouped)
    return tiles.reshape(layers, nb // per, MXU_DIM, MXU_DIM).astype(BF16)


def kernel(x, g_mix, g_ffn, g_final, w_in_even, w_out_even, lru_conv_w, lru_conv_b, lru_wa, lru_ba, lru_wi, lru_bi, lru_lambda, sconv_w, w_in_odd, w_out_odd, pool_w, pool_scale, hgrn_lb_logits, hgrn_norm_g, ffn_w_up, ffn_w_gate, ffn_conv_w, ffn_conv_b, ffn_w_down):
    depth = g_mix.shape[0]
    assert x.shape[1] % SEQ_TILE == 0
    rows = lambda v: v.reshape(v.shape[0], 1, -1).astype(F32)
    levels = jnp.asarray(_pair_levels())
    even = (rows(g_mix), _to_bf16(w_in_even), lru_conv_w, rows(lru_conv_b), _block_diag_tiles(lru_wa),
            _block_diag_tiles(lru_wi), rows(lru_ba), rows(lru_bi), rows(lru_lambda), sconv_w, _to_bf16(w_out_even))
    odd = (rows(g_mix), _to_bf16(w_in_odd), _block_diag_tiles(pool_w), rows(pool_scale),
           hgrn_lb_logits.astype(F32), rows(hgrn_norm_g), levels, _to_bf16(w_out_odd))
    ffn = (rows(g_ffn), _to_bf16(ffn_w_up), _to_bf16(ffn_w_gate), ffn_conv_w, rows(ffn_conv_b),
           _to_bf16(ffn_w_down), g_final.reshape(1, -1).astype(F32))
    x = _to_segment_major(x)
    for l in range(depth):
        if l % 2 == 0:
            x = _even_call(x, even, l)
        else:
            x = _odd_call(x, odd, l)
        x = _ffn_call(x, ffn, l, l == depth - 1)
    return _from_segment_major(x)
```

```python
import functools

import jax
import jax.numpy as jnp
import numpy as np
from jax import lax
from jax.experimental import pallas as pl
from jax.experimental.pallas import tpu as pltpu

F32 = jnp.float32
BF16 = jnp.bfloat16

SUBLANES = 8
LANES = 128
MXU_DIM = 256
VMEM_LIMIT_BYTES = 56 * 1024 * 1024

EPS = 1e-6
LOG2_E = 1.4426950408889634
MASKED_OUT = -1e30
LRU_C = 8.0
POOL_WINDOWS = (2, 4, 8, 16)

SEQ_TILE = 512
PT = 256
NSUB = SEQ_TILE // PT
SEG = PT // SUBLANES
SEG_BITS = SEG.bit_length() - 1
TIME_BITS = PT.bit_length() - 1
ROW_CHUNK = 64
CHUNKS_PER_BLOCK = PT // ROW_CHUNK
FFN_CHUNK = 256
PAIR_ROWS = 2 * SUBLANES
CAST_ROWS = 256


def _cparams():
    return pltpu.CompilerParams(
        dimension_semantics=("arbitrary", "arbitrary"),
        vmem_limit_bytes=VMEM_LIMIT_BYTES,
    )


def _sigmoid(v):
    return 0.5 * jnp.tanh(0.5 * v) + 0.5


def _rmsnorm_rows(x, g):
    ms = jnp.mean(x * x, axis=-1, keepdims=True)
    return x * lax.rsqrt(ms + EPS) * g


def _rmsnorm_to(x_ref, g_ref, h_ref, row0=0, nrows=SEQ_TILE):
    for r in range(nrows // ROW_CHUNK):
        rows = pl.ds(row0 + r * ROW_CHUNK, ROW_CHUNK)
        h_ref[rows, :] = _rmsnorm_rows(x_ref[rows, :], g_ref[...]).astype(h_ref.dtype)


def _fill_history(buf_ref, base, carry_ref, groups, lanes, carry_lanes=None):
    carry_lanes = lanes if carry_lanes is None else carry_lanes
    for g in range(groups):
        tail = buf_ref[pl.ds(base + PT + g * SUBLANES, SUBLANES), lanes]
        rolled = pltpu.roll(tail, 1, axis=0)
        sub = lax.broadcasted_iota(jnp.int32, rolled.shape, 0)
        prev = carry_ref[pl.ds(g * SUBLANES, SUBLANES), carry_lanes]
        buf_ref[pl.ds(base + g * SUBLANES, SUBLANES), lanes] = jnp.where(sub == 0, prev, rolled)
        carry_ref[pl.ds(g * SUBLANES, SUBLANES), carry_lanes] = rolled


def _row_time(t0, shape):
    row = lax.broadcasted_iota(jnp.int32, shape, 0)
    sub = row & (SUBLANES - 1)
    grp = row >> (SUBLANES.bit_length() - 1)
    return t0 + sub * SEG + grp


def _layer_spec(a, l):
    return pl.BlockSpec((None,) + a.shape[1:], lambda b, j: (l,) + (0,) * (a.ndim - 1))


def _whole_spec(a):
    return pl.BlockSpec(a.shape, lambda b, j: (0,) * a.ndim)


def _ffn_kernel(x_ref, g_ref, wup_ref, wgate_ref, cw_ref, cb_ref, wdown_ref, gfin_ref, o_ref,
                h_ref, up_ref, gate_ref, act_ref, carry_ref, *, final_norm):
    hist = 2
    hrows = hist * SUBLANES
    stride = hrows + PT
    ffn_dim = wup_ref.shape[1]

    @pl.when(pl.program_id(1) == 0)
    def _():
        carry_ref[...] = jnp.zeros_like(carry_ref)

    _rmsnorm_to(x_ref, g_ref, h_ref)

    for c in range(ffn_dim // FFN_CHUNK):
        cols = pl.ds(c * FFN_CHUNK, FFN_CHUNK)
        up = up_ref.at[c % 2]
        gate = gate_ref.at[c % 2]
        proj = jnp.dot(h_ref[...], wup_ref[:, cols], preferred_element_type=F32)
        for st in range(NSUB):
            up[pl.ds(st * stride + hrows, PT), :] = proj[st * PT:(st + 1) * PT, :]
        gate[...] = jnp.dot(h_ref[...], wgate_ref[:, cols], preferred_element_type=F32)
        w0 = cw_ref[0:1, cols]
        w1 = cw_ref[1:2, cols]
        w2 = cw_ref[2:3, cols]
        b = cb_ref[:, cols]
        for st in range(NSUB):
            _fill_history(up, st * stride, carry_ref, hist, slice(None), cols)
            for r in range(CHUNKS_PER_BLOCK):
                r0 = st * stride + r * ROW_CHUNK
                u = (up[pl.ds(r0 + hrows, ROW_CHUNK), :] * w2
                     + up[pl.ds(r0 + hrows - SUBLANES, ROW_CHUNK), :] * w1
                     + up[pl.ds(r0, ROW_CHUNK), :] * w0 + b)
                rows = pl.ds(st * PT + r * ROW_CHUNK, ROW_CHUNK)
                act_ref[rows, cols] = (jax.nn.gelu(u) * gate[rows, :]).astype(BF16)

    o_ref[...] = x_ref[...] + jnp.dot(act_ref[...], wdown_ref[...], preferred_element_type=F32)
    if final_norm:
        for r in range(SEQ_TILE // ROW_CHUNK):
            rows = pl.ds(r * ROW_CHUNK, ROW_CHUNK)
            o_ref[rows, :] = _rmsnorm_rows(o_ref[rows, :], gfin_ref[...])


def _ffn_call(x, params, layer, final_norm):
    bsz, seq, d = x.shape
    g, wup, wgate, cw, cb, wdown, gfin = params
    f = wup.shape[-1]
    xspec = pl.BlockSpec((None, SEQ_TILE, d), lambda b, j: (b, j, 0))
    return pl.pallas_call(
        functools.partial(_ffn_kernel, final_norm=final_norm),
        out_shape=jax.ShapeDtypeStruct(x.shape, x.dtype),
        grid=(bsz, seq // SEQ_TILE),
        in_specs=[xspec] + [_layer_spec(a, layer) for a in params[:-1]] + [_whole_spec(gfin)],
        out_specs=xspec,
        scratch_shapes=[
            pltpu.VMEM((SEQ_TILE, d), BF16),
            pltpu.VMEM((2, NSUB * (PT + 2 * SUBLANES), FFN_CHUNK), F32),
            pltpu.VMEM((2, SEQ_TILE, FFN_CHUNK), F32),
            pltpu.VMEM((SEQ_TILE, f), BF16),
            pltpu.VMEM((2 * SUBLANES, f), F32),
        ],
        compiler_params=_cparams(),
        name="conv_ffn",
    )(x, *params)


def _even_kernel(x_ref, g_ref, win_ref, cw_ref, cb_ref, wa_ref, wi_ref, ba_ref, bi_ref, lam_ref,
                 sw_ref, wout_ref, o_ref,
                 h_ref, *scratch):
    z_refs = scratch[:NSUB]
    xc_ref, xcb_ref, gate_ref, y_ref, zcarry_ref, hcarry_ref = scratch[NSUB:]
    hist = 3
    hrows = hist * SUBLANES
    w = xc_ref.shape[1]
    j = pl.program_id(1)

    @pl.when(j == 0)
    def _():
        zcarry_ref[...] = jnp.zeros_like(zcarry_ref)
        hcarry_ref[...] = jnp.zeros_like(hcarry_ref)

    xa_c, ga_c, hb_c, bg_c, cg_c = (pl.ds(k * w, w) for k in range(5))
    r_l, i_l = pl.ds(0, w), pl.ds(w, w)
    sub = lax.broadcasted_iota(jnp.int32, (SUBLANES, w), 0)

    lam = lam_ref[...]
    neg_c_sp = (-LRU_C * LOG2_E) * (jnp.maximum(-lam, 0.0) + jnp.log1p(jnp.exp(-jnp.abs(lam))))

    blk_rows = lambda st: pl.ds(st * PT, PT)
    chunk_rows = lambda st, r: pl.ds(st * PT + r * ROW_CHUNK, ROW_CHUNK)
    h_rep = {}

    def project(st, k):
        cols = pl.ds(k * w, w)
        z_refs[st][pl.ds(hrows, PT), cols] = jnp.dot(
            h_ref[blk_rows(st), :], win_ref[:, cols], preferred_element_type=F32)

    def history(st):
        for lanes in (xa_c, hb_c, cg_c):
            _fill_history(z_refs[st], 0, zcarry_ref, hist, lanes)

    def lru_conv(st, r):
        z_ref = z_refs[st]
        z0 = hrows + r * ROW_CHUNK
        acc = cb_ref[...] + z_ref[pl.ds(z0, ROW_CHUNK), xa_c] * cw_ref[3:4, :]
        for k in range(1, 4):
            acc = acc + z_ref[pl.ds(z0 - k * SUBLANES, ROW_CHUNK), xa_c] * cw_ref[3 - k:4 - k, :]
        xc_ref[chunk_rows(st, r), :] = acc
        xcb_ref[chunk_rows(st, r), :] = acc.astype(BF16)

    def gate_matmuls(st):
        for blk in range(w // MXU_DIM):
            lanes = pl.ds(blk * MXU_DIM, MXU_DIM)
            gate_ref[blk_rows(st), pl.ds(blk * MXU_DIM, MXU_DIM)] = jnp.dot(
                xcb_ref[blk_rows(st), lanes], wa_ref[blk], preferred_element_type=F32)
            gate_ref[blk_rows(st), pl.ds(w + blk * MXU_DIM, MXU_DIM)] = jnp.dot(
                xcb_ref[blk_rows(st), lanes], wi_ref[blk], preferred_element_type=F32)

    def gates(st, r):
        rows = chunk_rows(st, r)
        rg = _sigmoid(gate_ref[rows, r_l] + ba_ref[...])
        ig = _sigmoid(gate_ref[rows, i_l] + bi_ref[...])
        a = jnp.exp2(rg * neg_c_sp)
        mult = jnp.sqrt(1.0 - a * a)
        t = _row_time(j * SEQ_TILE + st * PT + r * (ROW_CHUNK // SUBLANES), mult.shape)
        mult = jnp.where(t == 0, 1.0, mult)
        xc_ref[rows, :] = mult * ig * xc_ref[rows, :]
        gate_ref[rows, r_l] = a

    def scan(st):
        h = jnp.zeros((SUBLANES, w), F32)
        p = h + 1.0
        for i in range(SEG):
            rows = pl.ds(st * PT + i * SUBLANES, SUBLANES)
            a = gate_ref[rows, r_l]
            h = a * h + xc_ref[rows, :]
            p = a * p
            xc_ref[rows, :] = h
            gate_ref[rows, r_l] = p
        h_in = jnp.where(sub == 0, hcarry_ref[...], 0.0)
        for s in range(SUBLANES - 1):
            nxt = pltpu.roll(h + p * h_in, 1, axis=0)
            h_in = jnp.where(sub == s + 1, nxt, h_in)
        hcarry_ref[...] = pltpu.roll(h + p * h_in, 1, axis=0)
        h_rep[st] = jnp.concatenate([h_in] * (ROW_CHUNK // SUBLANES), axis=0)

    def outputs(st, r):
        z_ref = z_refs[st]
        rows = chunk_rows(st, r)
        z0 = hrows + r * ROW_CHUNK
        hs = xc_ref[rows, :] + gate_ref[rows, r_l] * h_rep[st]
        ya = hs * jax.nn.gelu(z_ref[pl.ds(z0, ROW_CHUNK), ga_c])
        conv = None
        for k in range(3):
            zrows = pl.ds(z0 - k * SUBLANES, ROW_CHUNK)
            term = z_ref[zrows, cg_c] * z_ref[zrows, hb_c] * sw_ref[2 - k:3 - k, :]
            conv = term if conv is None else conv + term
        yb = z_ref[pl.ds(z0, ROW_CHUNK), bg_c] * conv
        y_ref[rows, pl.ds(0, w)] = ya.astype(BF16)
        y_ref[rows, pl.ds(w, w)] = yb.astype(BF16)

    def out_project(st, n):
        cols = pl.ds(n * MXU_DIM, MXU_DIM)
        o_ref[blk_rows(st), cols] = x_ref[blk_rows(st), cols] + jnp.dot(
            y_ref[blk_rows(st), :], wout_ref[:, cols], preferred_element_type=F32)

    d_model = x_ref.shape[1]
    n_out = d_model // MXU_DIM
    _rmsnorm_to(x_ref, g_ref, h_ref, 0, PT)
    for k in range(5):
        project(0, k)
    for st in range(NSUB):
        nxt = st + 1 if st + 1 < NSUB else None
        prv = st - 1 if st > 0 else None
        pieces = []
        if nxt is not None:
            pieces += [functools.partial(project, nxt, k) for k in range(5)]
        if prv is not None:
            pieces += [functools.partial(out_project, prv, n) for n in range(n_out)]
        vector_steps = ([functools.partial(lru_conv, st, r) for r in range(CHUNKS_PER_BLOCK)]
                        + [functools.partial(gate_matmuls, st)]
                        + [functools.partial(gates, st, r) for r in range(CHUNKS_PER_BLOCK)]
                        + [functools.partial(scan, st)]
                        + [functools.partial(outputs, st, r) for r in range(CHUNKS_PER_BLOCK)])
        history(st)
        if nxt is not None:
            _rmsnorm_to(x_ref, g_ref, h_ref, nxt * PT, PT)
        emitted = 0
        for idx, step in enumerate(vector_steps):
            step()
            due = (idx + 1) * len(pieces) // len(vector_steps)
            while emitted < due:
                pieces[emitted]()
                emitted += 1
    for n in range(n_out):
        out_project(NSUB - 1, n)


def _even_call(x, params, layer):
    bsz, seq, d = x.shape
    zin = params[1].shape[-1]
    w = zin // 5
    xspec = pl.BlockSpec((None, SEQ_TILE, d), lambda b, j: (b, j, 0))
    specs = [_layer_spec(params[0], layer)] + [_layer_spec(a, layer // 2) for a in params[1:]]
    return pl.pallas_call(
        _even_kernel,
        out_shape=jax.ShapeDtypeStruct(x.shape, x.dtype),
        grid=(bsz, seq // SEQ_TILE),
        in_specs=[xspec] + specs,
        out_specs=xspec,
        scratch_shapes=[
            pltpu.VMEM((SEQ_TILE, d), BF16),
        ] + [pltpu.VMEM((PT + 3 * SUBLANES, zin), F32) for _ in range(NSUB)] + [
            pltpu.VMEM((SEQ_TILE, w), F32),
            pltpu.VMEM((SEQ_TILE, w), BF16),
            pltpu.VMEM((SEQ_TILE, 2 * w), F32),
            pltpu.VMEM((SEQ_TILE, 2 * w), BF16),
            pltpu.VMEM((3 * SUBLANES, zin), F32),
            pltpu.VMEM((SUBLANES, w), F32),
        ],
        compiler_params=_cparams(),
        name="lru_sconv_mixer",
    )(x, *params)


def _pair_levels():
    p = np.arange(PT)
    t = (p % SUBLANES) * SEG + p // SUBLANES
    x = t[:, None] ^ t[None, :]
    lvl = np.full(x.shape, -1.0, np.float32)
    for b in range(TIME_BITS):
        lvl[(x >> b) == 1] = b
    return lvl


def _sublane_row(v, s):
    return jnp.broadcast_to(v[s:s + 1, :], v.shape)


def _odd_kernel(x_ref, g_ref, win_ref, pw_ref, pscale_ref, lbl_ref, ng_ref, lvl_ref, wout_ref, o_ref,
                h_ref, z_ref, pa_ref, pb_ref, pcarry_ref, cum_ref, k_ref, lvq_ref, lvk_ref, qh_ref, kh_ref,
                vb_ref, oh_ref, y_ref, state_ref, *, layer):
    w = pscale_ref.shape[1]
    heads = w // LANES
    j = pl.program_id(1)
    hist = POOL_WINDOWS[-1] // 2
    hrows = hist * SUBLANES

    @pl.when(jnp.logical_and(pl.program_id(0) == 0, j == 0))
    def _():
        lvq_ref[...] = jnp.zeros_like(lvq_ref)
        lvk_ref[...] = jnp.zeros_like(lvk_ref)

    @pl.when(j == 0)
    def _():
        pcarry_ref[...] = jnp.zeros_like(pcarry_ref)
        state_ref[...] = jnp.zeros_like(state_ref)

    uc_c, q_c, fz_c, v_c, gd_c = (k * w for k in range(5))

    logits = lbl_ref[...]
    sm = jnp.exp(logits - jnp.max(logits, axis=0, keepdims=True))
    sm = sm / jnp.sum(sm, axis=0, keepdims=True)
    lb = jnp.zeros((1, w), F32)
    for o in range(1, layer + 1):
        lb = lb + sm[o:o + 1, :]
    log_lb = jnp.log(lb)
    log_1m_lb = jnp.log1p(-lb)

    sub8 = lax.broadcasted_iota(jnp.int32, (SUBLANES, LANES), 0)
    row16 = lax.broadcasted_iota(jnp.int32, (PAIR_ROWS, LANES), 0)
    first_masked = jnp.where(row16 < SUBLANES, MASKED_OUT, 0.0)
    nt = (((1,), (1,)), ((), ()))
    tn = (((0,), (0,)), ((), ()))
    twice = lambda v: jnp.concatenate([v, v], axis=0)
    bufs = (pa_ref, pb_ref)

    blk_rows = lambda st: pl.ds(st * PT, PT)

    def pool(st):
        base = st * PT
        pa_ref[pl.ds(hrows, PT), :] = z_ref[blk_rows(st), pl.ds(uc_c, w)]
        for stage in range(len(POOL_WINDOWS)):
            shift = 1 << stage
            src = bufs[stage % 2]
            dst = bufs[(stage + 1) % 2]
            lanes = pl.ds(stage * LANES, w - stage * LANES)
            for gidx in range(shift):
                tail = src[pl.ds(hrows + PT - (shift - gidx) * SUBLANES, SUBLANES), lanes]
                rolled = pltpu.roll(tail, 1, axis=0)
                crow = pl.ds((shift - 1 + gidx) * SUBLANES, SUBLANES)
                prev = pcarry_ref[crow, lanes]
                src[pl.ds(hrows - (shift - gidx) * SUBLANES, SUBLANES), lanes] = jnp.where(
                    lax.broadcasted_iota(jnp.int32, rolled.shape, 0) == 0, prev, rolled)
                pcarry_ref[crow, lanes] = rolled
            for r in range(CHUNKS_PER_BLOCK):
                r0 = r * ROW_CHUNK + hrows
                dst[pl.ds(r0, ROW_CHUNK), lanes] = (
                    src[pl.ds(r0, ROW_CHUNK), lanes] + src[pl.ds(r0 - shift * SUBLANES, ROW_CHUNK), lanes])

        for r in range(CHUNKS_PER_BLOCK):
            t = _row_time(j * SEQ_TILE + base + r * (ROW_CHUNK // SUBLANES), (ROW_CHUNK, LANES))
            for gi, win in enumerate(POOL_WINDOWS):
                buf = bufs[(gi + 1) % 2]
                lanes = pl.ds(gi * LANES, LANES)
                cnt = jnp.minimum(t + 1, win).astype(F32)
                rows = pl.ds(base + r * ROW_CHUNK, ROW_CHUNK)
                p = buf[pl.ds(hrows + r * ROW_CHUNK, ROW_CHUNK), lanes] / cnt - z_ref[rows, pl.ds(uc_c + gi * LANES, LANES)]
                y_ref[rows, lanes] = p.astype(BF16)

        for blk in range(w // MXU_DIM):
            lanes = pl.ds(blk * MXU_DIM, MXU_DIM)
            mixed = jnp.dot(y_ref[blk_rows(st), lanes], pw_ref[blk], preferred_element_type=F32)
            y_ref[blk_rows(st), lanes] = (mixed * pscale_ref[:, lanes]).astype(BF16)

    totals = {}

    def head_factors(st, hd):
        base = st * PT
        par = hd % 2
        cum_b, k_b, lvq, lvk = cum_ref.at[par], k_ref.at[par], lvq_ref.at[par], lvk_ref.at[par]
        qh_b, kh_b, vb_b = qh_ref.at[par], kh_ref.at[par], vb_ref.at[par]
        la = log_lb[:, hd * LANES:(hd + 1) * LANES]
        l1 = log_1m_lb[:, hd * LANES:(hd + 1) * LANES]
        om = 1.0 - lb[:, hd * LANES:(hd + 1) * LANES]
        q_l = pl.ds(q_c + hd * LANES, LANES)
        fz_l = pl.ds(fz_c + hd * LANES, LANES)
        v_l = pl.ds(v_c + hd * LANES, LANES)

        cum = jnp.zeros((SUBLANES, LANES), F32)
        for r in range(CHUNKS_PER_BLOCK):
            fz = z_ref[pl.ds(base + r * ROW_CHUNK, ROW_CHUNK), fz_l]
            e = jnp.exp(-jnp.abs(fz))
            log_sig = jnp.minimum(fz, 0.0) - jnp.log(1.0 + e)
            b = l1 + log_sig
            log_f = LOG2_E * (jnp.maximum(la, b) + jnp.log(1.0 + jnp.exp(-jnp.abs(la - b))))
            rcp = 1.0 / (1.0 + e)
            k_b[pl.ds(r * ROW_CHUNK, ROW_CHUNK), :] = om * jnp.where(fz >= 0, e * rcp, rcp)
            parts = []
            for g in range(ROW_CHUNK // SUBLANES):
                cum = cum + log_f[g * SUBLANES:(g + 1) * SUBLANES, :]
                parts.append(cum)
            cum_b[pl.ds(r * ROW_CHUNK, ROW_CHUNK), :] = jnp.concatenate(parts, axis=0)
        cum_last = cum

        incl = cum_last
        for sh in (1, 2, 4):
            incl = incl + jnp.where(sub8 >= sh, pltpu.roll(incl, sh, axis=0), 0.0)
        offset = incl - cum_last
        total = _sublane_row(incl, SUBLANES - 1)
        totals[st, hd] = total

        seg_levels = []
        for c in range(TIME_BITS - SEG_BITS):
            upper8 = ((sub8 >> c) & 1) == 1
            incl_mid = jnp.zeros_like(incl)
            for s_mid in sorted({((s >> (c + 1)) << (c + 1)) | ((1 << c) - 1) for s in range(SUBLANES)}):
                in_block = (sub8 >> (c + 1)) == (s_mid >> (c + 1))
                incl_mid = jnp.where(in_block, _sublane_row(incl, s_mid), incl_mid)
            q_add = jnp.where(upper8, offset - incl_mid, MASKED_OUT)
            k_add = jnp.where(upper8, MASKED_OUT, incl_mid - incl + cum_last)
            seg_levels.append((twice(q_add), twice(k_add)))
        q_const = twice(offset)
        k_const = twice(cum_last + total - incl)

        for m in range(SEG // 2):
            rows = pl.ds(m * PAIR_ROWS, PAIR_ROWS)
            q = z_ref[pl.ds(base + m * PAIR_ROWS, PAIR_ROWS), q_l]
            k = k_b[rows, :]
            cum = cum_b[rows, :]
            for b in range(SEG_BITS):
                if b == 0:
                    dec = jnp.exp2(cum - twice(cum[0:SUBLANES, :]) + first_masked)
                    lvq[0, rows, :] = (q * dec).astype(BF16)
                    lvk[0, rows, :] = jnp.where(row16 < SUBLANES, k, 0.0).astype(BF16)
                    continue
                mid = ((m >> b) << (b + 1)) | ((1 << b) - 1)
                ref = twice(cum_b[pl.ds(mid * SUBLANES, SUBLANES), :])
                if (m >> (b - 1)) & 1:
                    lvq[b, rows, :] = (q * jnp.exp2(cum - ref)).astype(BF16)
                else:
                    lvk[b, rows, :] = (k * jnp.exp2(ref - cum)).astype(BF16)
            ncum = -cum
            for b in range(SEG_BITS, TIME_BITS):
                q_add, k_add = seg_levels[b - SEG_BITS]
                lvq[b, rows, :] = (q * jnp.exp2(cum + q_add)).astype(BF16)
                lvk[b, rows, :] = (k * jnp.exp2(ncum + k_add)).astype(BF16)
            qh_b[rows, :] = (q * jnp.exp2(cum + q_const)).astype(BF16)
            kh_b[rows, :] = (k * jnp.exp2(ncum + k_const)).astype(BF16)
        vb_b[...] = z_ref[blk_rows(st), v_l].astype(BF16)

    def head_products(st, hd):
        par = hd % 2
        k_b, lvq, lvk = k_ref.at[par], lvq_ref.at[par], lvk_ref.at[par]
        qh_b, kh_b, vb_b = qh_ref.at[par], kh_ref.at[par], vb_ref.at[par]
        q_l = pl.ds(q_c + hd * LANES, LANES)
        v_l = pl.ds(v_c + hd * LANES, LANES)
        att = jnp.zeros((PT, PT), F32)
        for b in range(TIME_BITS):
            part = lax.dot_general(lvq[b], lvk[b], nt, preferred_element_type=F32)
            att = jnp.where(lvl_ref[...] == float(b), part, att)

        state = state_ref[hd]
        diag = jnp.sum(z_ref[blk_rows(st), q_l] * k_b[...], axis=-1, keepdims=True)
        o = jnp.dot(att.astype(BF16), vb_b[...], preferred_element_type=F32)
        o = o + lax.dot_general(qh_b[...], state.astype(BF16), nt, preferred_element_type=F32)
        oh_ref[hd, blk_rows(st), :] = o + diag * z_ref[blk_rows(st), v_l]
        upd = lax.dot_general(vb_b[...], kh_b[...], tn, preferred_element_type=F32)
        state_ref[hd] = state * jnp.exp2(totals[st, hd][0:1, :]) + upd

    def head_outputs(st, r):
        rows = pl.ds(st * PT + r * ROW_CHUNK, ROW_CHUNK)
        for hd in range(heads):
            o = oh_ref[hd, rows, :]
            o = o * lax.rsqrt(jnp.mean(o * o, axis=-1, keepdims=True) + EPS) * ng_ref[:, pl.ds(hd * LANES, LANES)]
            gd = z_ref[rows, pl.ds(gd_c + hd * LANES, LANES)]
            y_ref[rows, pl.ds(w + hd * LANES, LANES)] = (o * (gd * _sigmoid(gd))).astype(BF16)

    _rmsnorm_to(x_ref, g_ref, h_ref)
    z_ref[...] = jnp.dot(h_ref[...], win_ref[...], preferred_element_type=F32)
    for st in range(NSUB):
        pool(st)
        for hd in range(heads):
            head_factors(st, hd)
            head_products(st, hd)
        for r in range(CHUNKS_PER_BLOCK):
            head_outputs(st, r)
    o_ref[...] = x_ref[...] + jnp.dot(y_ref[...], wout_ref[...], preferred_element_type=F32)


def _odd_call(x, params, layer):
    bsz, seq, d = x.shape
    g, win, pw, pscale, lbl, ng, levels, wout = params
    zin = win.shape[-1]
    w = zin // 5
    heads = w // LANES
    hist = POOL_WINDOWS[-1] // 2
    o = layer // 2
    xspec = pl.BlockSpec((None, SEQ_TILE, d), lambda b, j: (b, j, 0))
    specs = [_layer_spec(g, layer), _layer_spec(win, o), _layer_spec(pw, o), _layer_spec(pscale, o),
             _whole_spec(lbl), _layer_spec(ng, o), _whole_spec(levels), _layer_spec(wout, o)]
    return pl.pallas_call(
        functools.partial(_odd_kernel, layer=o),
        out_shape=jax.ShapeDtypeStruct(x.shape, x.dtype),
        grid=(bsz, seq // SEQ_TILE),
        in_specs=[xspec] + specs,
        out_specs=xspec,
        scratch_shapes=[
            pltpu.VMEM((SEQ_TILE, d), BF16),
            pltpu.VMEM((SEQ_TILE, zin), F32),
            pltpu.VMEM((PT + hist * SUBLANES, w), F32),
            pltpu.VMEM((PT + hist * SUBLANES, w), F32),
            pltpu.VMEM(((POOL_WINDOWS[-1] - 1) * SUBLANES, w), F32),
            pltpu.VMEM((2, PT, LANES), F32),
            pltpu.VMEM((2, PT, LANES), F32),
            pltpu.VMEM((2, TIME_BITS, PT, LANES), BF16),
            pltpu.VMEM((2, TIME_BITS, PT, LANES), BF16),
            pltpu.VMEM((2, PT, LANES), BF16),
            pltpu.VMEM((2, PT, LANES), BF16),
            pltpu.VMEM((2, PT, LANES), BF16),
            pltpu.VMEM((heads, SEQ_TILE, LANES), F32),
            pltpu.VMEM((SEQ_TILE, 2 * w), BF16),
            pltpu.VMEM((heads, LANES, LANES), F32),
        ],
        compiler_params=_cparams(),
        name="pool_hgrn_mixer",
    )(x, *params)


def _cast_kernel(x_ref, o_ref):
    o_ref[...] = x_ref[...].astype(o_ref.dtype)


def _to_bf16(wstack):
    rows = int(np.prod(wstack.shape[:-1]))
    cols = wstack.shape[-1]
    assert rows % CAST_ROWS == 0
    spec = pl.BlockSpec((CAST_ROWS, cols), lambda i: (i, 0))
    out = pl.pallas_call(
        _cast_kernel,
        out_shape=jax.ShapeDtypeStruct((rows, cols), BF16),
        grid=(rows // CAST_ROWS,),
        in_specs=[spec],
        out_specs=spec,
        name="weights_to_bf16",
    )(wstack.reshape(rows, cols))
    return out.reshape(wstack.shape)


def _to_segment_major(x):
    b, s, d = x.shape
    return x.reshape(b, s // PT, SUBLANES, SEG, d).transpose(0, 1, 3, 2, 4).reshape(b, s, d)


def _from_segment_major(x):
    b, s, d = x.shape
    return x.reshape(b, s // PT, SEG, SUBLANES, d).transpose(0, 1, 3, 2, 4).reshape(b, s, d)


def _block_diag_tiles(wblocks):
    layers, nb, n, _ = wblocks.shape
    per = MXU_DIM // n
    grouped = wblocks.reshape(layers, nb // per, per, n, n)
    tiles = jnp.einsum("ab,ltaij->ltaibj", jnp.eye(per, dtype=wblocks.dtype), grouped)
    return tiles.reshape(layers, nb // per, MXU_DIM, MXU_DIM).astype(BF16)


def kernel(x, g_mix, g_ffn, g_final, w_in_even, w_out_even, lru_conv_w, lru_conv_b, lru_wa, lru_ba, lru_wi, lru_bi, lru_lambda, sconv_w, w_in_odd, w_out_odd, pool_w, pool_scale, hgrn_lb_logits, hgrn_norm_g, ffn_w_up, ffn_w_gate, ffn_conv_w, ffn_conv_b, ffn_w_down):
    depth = g_mix.shape[0]
    assert x.shape[1] % SEQ_TILE == 0
    rows = lambda v: v.reshape(v.shape[0], 1, -1).astype(F32)
    levels = jnp.asarray(_pair_levels())
    even = (rows(g_mix), _to_bf16(w_in_even), lru_conv_w, rows(lru_conv_b), _block_diag_tiles(lru_wa),
            _block_diag_tiles(lru_wi), rows(lru_ba), rows(lru_bi), rows(lru_lambda), sconv_w, _to_bf16(w_out_even))
    odd = (rows(g_mix), _to_bf16(w_in_odd), _block_diag_tiles(pool_w), rows(pool_scale),
           hgrn_lb_logits.astype(F32), rows(hgrn_norm_g), levels, _to_bf16(w_out_odd))
    ffn = (rows(g_ffn), _to_bf16(ffn_w_up), _to_bf16(ffn_w_gate), ffn_conv_w, rows(ffn_conv_b),
           _to_bf16(ffn_w_down), g_final.reshape(1, -1).astype(F32))
    x = _to_segment_major(x)
    for l in range(depth):
        if l % 2 == 0:
            x = _even_call(x, even, l)
        else:
            x = _odd_call(x, odd, l)
        x = _ffn_call(x, ffn, l, l == depth - 1)
    return _from_segment_major(x)
```

```python
import functools

import jax
import jax.numpy as jnp
import numpy as np
from jax import lax
from jax.experimental import pallas as pl
from jax.experimental.pallas import tpu as pltpu

F32 = jnp.float32
BF16 = jnp.bfloat16

SUBLANES = 8
LANES = 128
MXU_DIM = 256
VMEM_LIMIT_BYTES = 56 * 1024 * 1024

EPS = 1e-6
LOG2_E = 1.4426950408889634
MASKED_OUT = -1e30
LRU_C = 8.0
POOL_WINDOWS = (2, 4, 8, 16)

SEQ_TILE = 512
PT = 256
NSUB = SEQ_TILE // PT
SEG = PT // SUBLANES
SEG_BITS = SEG.bit_length() - 1
TIME_BITS = PT.bit_length() - 1
ROW_CHUNK = 64
CHUNKS_PER_BLOCK = PT // ROW_CHUNK
FFN_CHUNK = 256
PAIR_ROWS = 2 * SUBLANES
CAST_ROWS = 512


def _cparams():
    return pltpu.CompilerParams(
        dimension_semantics=("arbitrary", "arbitrary"),
        vmem_limit_bytes=VMEM_LIMIT_BYTES,
    )


def _sigmoid(v):
    return 0.5 * jnp.tanh(0.5 * v) + 0.5


def _rmsnorm_rows(x, g):
    ms = jnp.mean(x * x, axis=-1, keepdims=True)
    return x * lax.rsqrt(ms + EPS) * g


def _rmsnorm_to(x_ref, g_ref, h_ref, row0=0, nrows=SEQ_TILE):
    for r in range(nrows // ROW_CHUNK):
        rows = pl.ds(row0 + r * ROW_CHUNK, ROW_CHUNK)
        h_ref[rows, :] = _rmsnorm_rows(x_ref[rows, :], g_ref[...]).astype(h_ref.dtype)


def _fill_history(buf_ref, base, carry_ref, groups, lanes, carry_lanes=None):
    carry_lanes = lanes if carry_lanes is None else carry_lanes
    for g in range(groups):
        tail = buf_ref[pl.ds(base + PT + g * SUBLANES, SUBLANES), lanes]
        rolled = pltpu.roll(tail, 1, axis=0)
        sub = lax.broadcasted_iota(jnp.int32, rolled.shape, 0)
        prev = carry_ref[pl.ds(g * SUBLANES, SUBLANES), carry_lanes]
        buf_ref[pl.ds(base + g * SUBLANES, SUBLANES), lanes] = jnp.where(sub == 0, prev, rolled)
        carry_ref[pl.ds(g * SUBLANES, SUBLANES), carry_lanes] = rolled


def _row_time(t0, shape):
    row = lax.broadcasted_iota(jnp.int32, shape, 0)
    sub = row & (SUBLANES - 1)
    grp = row >> (SUBLANES.bit_length() - 1)
    return t0 + sub * SEG + grp


def _layer_spec(a, l):
    return pl.BlockSpec((None,) + a.shape[1:], lambda b, j: (l,) + (0,) * (a.ndim - 1))


def _whole_spec(a):
    return pl.BlockSpec(a.shape, lambda b, j: (0,) * a.ndim)


def _ffn_kernel(x_ref, g_ref, wup_ref, wgate_ref, cw_ref, cb_ref, wdown_ref, gfin_ref, o_ref,
                h_ref, up_ref, gate_ref, act_ref, carry_ref, *, final_norm):
    hist = 2
    hrows = hist * SUBLANES
    stride = hrows + PT
    ffn_dim = wup_ref.shape[1]

    @pl.when(pl.program_id(1) == 0)
    def _():
        carry_ref[...] = jnp.zeros_like(carry_ref)

    _rmsnorm_to(x_ref, g_ref, h_ref)

    for c in range(ffn_dim // FFN_CHUNK):
        cols = pl.ds(c * FFN_CHUNK, FFN_CHUNK)
        up = up_ref.at[c % 2]
        gate = gate_ref.at[c % 2]
        proj = jnp.dot(h_ref[...], wup_ref[:, cols], preferred_element_type=F32)
        for st in range(NSUB):
            up[pl.ds(st * stride + hrows, PT), :] = proj[st * PT:(st + 1) * PT, :]
        gate[...] = jnp.dot(h_ref[...], wgate_ref[:, cols], preferred_element_type=F32)
        w0 = cw_ref[0:1, cols]
        w1 = cw_ref[1:2, cols]
        w2 = cw_ref[2:3, cols]
        b = cb_ref[:, cols]
        for st in range(NSUB):
            _fill_history(up, st * stride, carry_ref, hist, slice(None), cols)
            for r in range(CHUNKS_PER_BLOCK):
                r0 = st * stride + r * ROW_CHUNK
                u = (up[pl.ds(r0 + hrows, ROW_CHUNK), :] * w2
                     + up[pl.ds(r0 + hrows - SUBLANES, ROW_CHUNK), :] * w1
                     + up[pl.ds(r0, ROW_CHUNK), :] * w0 + b)
                rows = pl.ds(st * PT + r * ROW_CHUNK, ROW_CHUNK)
                act_ref[rows, cols] = (jax.nn.gelu(u) * gate[rows, :]).astype(BF16)

    o_ref[...] = x_ref[...] + jnp.dot(act_ref[...], wdown_ref[...], preferred_element_type=F32)
    if final_norm:
        for r in range(SEQ_TILE // ROW_CHUNK):
            rows = pl.ds(r * ROW_CHUNK, ROW_CHUNK)
            o_ref[rows, :] = _rmsnorm_rows(o_ref[rows, :], gfin_ref[...])


def _ffn_call(x, params, layer, final_norm):
    bsz, seq, d = x.shape
    g, wup, wgate, cw, cb, wdown, gfin = params
    f = wup.shape[-1]
    xspec = pl.BlockSpec((None, SEQ_TILE, d), lambda b, j: (b, j, 0))
    return pl.pallas_call(
        functools.partial(_ffn_kernel, final_norm=final_norm),
        out_shape=jax.ShapeDtypeStruct(x.shape, x.dtype),
        grid=(bsz, seq // SEQ_TILE),
        in_specs=[xspec] + [_layer_spec(a, layer) for a in params[:-1]] + [_whole_spec(gfin)],
        out_specs=xspec,
        scratch_shapes=[
            pltpu.VMEM((SEQ_TILE, d), BF16),
            pltpu.VMEM((2, NSUB * (PT + 2 * SUBLANES), FFN_CHUNK), F32),
            pltpu.VMEM((2, SEQ_TILE, FFN_CHUNK), F32),
            pltpu.VMEM((SEQ_TILE, f), BF16),
            pltpu.VMEM((2 * SUBLANES, f), F32),
        ],
        compiler_params=_cparams(),
        name="conv_ffn",
    )(x, *params)


def _even_kernel(x_ref, g_ref, win_ref, cw_ref, cb_ref, wa_ref, wi_ref, ba_ref, bi_ref, lam_ref,
                 sw_ref, wout_ref, o_ref,
                 h_ref, *scratch):
    z_refs = scratch[:NSUB]
    xc_ref, xcb_ref, gate_ref, y_ref, zcarry_ref, hcarry_ref = scratch[NSUB:]
    hist = 3
    hrows = hist * SUBLANES
    w = xc_ref.shape[1]
    j = pl.program_id(1)

    @pl.when(j == 0)
    def _():
        zcarry_ref[...] = jnp.zeros_like(zcarry_ref)
        hcarry_ref[...] = jnp.zeros_like(hcarry_ref)

    xa_c, ga_c, hb_c, bg_c, cg_c = (pl.ds(k * w, w) for k in range(5))
    r_l, i_l = pl.ds(0, w), pl.ds(w, w)
    sub = lax.broadcasted_iota(jnp.int32, (SUBLANES, w), 0)

    lam = lam_ref[...]
    neg_c_sp = (-LRU_C * LOG2_E) * (jnp.maximum(-lam, 0.0) + jnp.log1p(jnp.exp(-jnp.abs(lam))))

    blk_rows = lambda st: pl.ds(st * PT, PT)
    chunk_rows = lambda st, r: pl.ds(st * PT + r * ROW_CHUNK, ROW_CHUNK)
    h_rep = {}

    def project(st, k):
        cols = pl.ds(k * w, w)
        z_refs[st][pl.ds(hrows, PT), cols] = jnp.dot(
            h_ref[blk_rows(st), :], win_ref[:, cols], preferred_element_type=F32)

    def history(st):
        for lanes in (xa_c, hb_c, cg_c):
            _fill_history(z_refs[st], 0, zcarry_ref, hist, lanes)

    def lru_conv(st, r):
        z_ref = z_refs[st]
        z0 = hrows + r * ROW_CHUNK
        acc = cb_ref[...] + z_ref[pl.ds(z0, ROW_CHUNK), xa_c] * cw_ref[3:4, :]
        for k in range(1, 4):
            acc = acc + z_ref[pl.ds(z0 - k * SUBLANES, ROW_CHUNK), xa_c] * cw_ref[3 - k:4 - k, :]
        xc_ref[chunk_rows(st, r), :] = acc
        xcb_ref[chunk_rows(st, r), :] = acc.astype(BF16)

    def gate_matmuls(st):
        for blk in range(w // MXU_DIM):
            lanes = pl.ds(blk * MXU_DIM, MXU_DIM)
            gate_ref[blk_rows(st), pl.ds(blk * MXU_DIM, MXU_DIM)] = jnp.dot(
                xcb_ref[blk_rows(st), lanes], wa_ref[blk], preferred_element_type=F32)
            gate_ref[blk_rows(st), pl.ds(w + blk * MXU_DIM, MXU_DIM)] = jnp.dot(
                xcb_ref[blk_rows(st), lanes], wi_ref[blk], preferred_element_type=F32)

    def gates(st, r):
        rows = chunk_rows(st, r)
        rg = _sigmoid(gate_ref[rows, r_l] + ba_ref[...])
        ig = _sigmoid(gate_ref[rows, i_l] + bi_ref[...])
        a = jnp.exp2(rg * neg_c_sp)
        mult = jnp.sqrt(1.0 - a * a)
        t = _row_time(j * SEQ_TILE + st * PT + r * (ROW_CHUNK // SUBLANES), mult.shape)
        mult = jnp.where(t == 0, 1.0, mult)
        xc_ref[rows, :] = mult * ig * xc_ref[rows, :]
        gate_ref[rows, r_l] = a

    def scan(st):
        h = jnp.zeros((SUBLANES, w), F32)
        p = h + 1.0
        for i in range(SEG):
            rows = pl.ds(st * PT + i * SUBLANES, SUBLANES)
            a = gate_ref[rows, r_l]
            h = a * h + xc_ref[rows, :]
            p = a * p
            xc_ref[rows, :] = h
            gate_ref[rows, r_l] = p
        h_in = jnp.where(sub == 0, hcarry_ref[...], 0.0)
        for s in range(SUBLANES - 1):
            nxt = pltpu.roll(h + p * h_in, 1, axis=0)
            h_in = jnp.where(sub == s + 1, nxt, h_in)
        hcarry_ref[...] = pltpu.roll(h + p * h_in, 1, axis=0)
        h_rep[st] = jnp.concatenate([h_in] * (ROW_CHUNK // SUBLANES), axis=0)

    def outputs(st, r):
        z_ref = z_refs[st]
        rows = chunk_rows(st, r)
        z0 = hrows + r * ROW_CHUNK
        hs = xc_ref[rows, :] + gate_ref[rows, r_l] * h_rep[st]
        ya = hs * jax.nn.gelu(z_ref[pl.ds(z0, ROW_CHUNK), ga_c])
        conv = None
        for k in range(3):
            zrows = pl.ds(z0 - k * SUBLANES, ROW_CHUNK)
            term = z_ref[zrows, cg_c] * z_ref[zrows, hb_c] * sw_ref[2 - k:3 - k, :]
            conv = term if conv is None else conv + term
        yb = z_ref[pl.ds(z0, ROW_CHUNK), bg_c] * conv
        y_ref[rows, pl.ds(0, w)] = ya.astype(BF16)
        y_ref[rows, pl.ds(w, w)] = yb.astype(BF16)

    def out_project(st, n):
        cols = pl.ds(n * MXU_DIM, MXU_DIM)
        o_ref[blk_rows(st), cols] = x_ref[blk_rows(st), cols] + jnp.dot(
            y_ref[blk_rows(st), :], wout_ref[:, cols], preferred_element_type=F32)

    d_model = x_ref.shape[1]
    n_out = d_model // MXU_DIM
    _rmsnorm_to(x_ref, g_ref, h_ref, 0, PT)
    for k in range(5):
        project(0, k)
    for st in range(NSUB):
        nxt = st + 1 if st + 1 < NSUB else None
        prv = st - 1 if st > 0 else None
        pieces = []
        if nxt is not None:
            pieces += [functools.partial(project, nxt, k) for k in range(5)]
        if prv is not None:
            pieces += [functools.partial(out_project, prv, n) for n in range(n_out)]
        vector_steps = ([functools.partial(lru_conv, st, r) for r in range(CHUNKS_PER_BLOCK)]
                        + [functools.partial(gate_matmuls, st)]
                        + [functools.partial(gates, st, r) for r in range(CHUNKS_PER_BLOCK)]
                        + [functools.partial(scan, st)]
                        + [functools.partial(outputs, st, r) for r in range(CHUNKS_PER_BLOCK)])
        history(st)
        if nxt is not None:
            _rmsnorm_to(x_ref, g_ref, h_ref, nxt * PT, PT)
        emitted = 0
        for idx, step in enumerate(vector_steps):
            step()
            due = (idx + 1) * len(pieces) // len(vector_steps)
            while emitted < due:
                pieces[emitted]()
                emitted += 1
    for n in range(n_out):
        out_project(NSUB - 1, n)


def _even_call(x, params, layer):
    bsz, seq, d = x.shape
    zin = params[1].shape[-1]
    w = zin // 5
    xspec = pl.BlockSpec((None, SEQ_TILE, d), lambda b, j: (b, j, 0))
    specs = [_layer_spec(params[0], layer)] + [_layer_spec(a, layer // 2) for a in params[1:]]
    return pl.pallas_call(
        _even_kernel,
        out_shape=jax.ShapeDtypeStruct(x.shape, x.dtype),
        grid=(bsz, seq // SEQ_TILE),
        in_specs=[xspec] + specs,
        out_specs=xspec,
        scratch_shapes=[
            pltpu.VMEM((SEQ_TILE, d), BF16),
        ] + [pltpu.VMEM((PT + 3 * SUBLANES, zin), F32) for _ in range(NSUB)] + [
            pltpu.VMEM((SEQ_TILE, w), F32),
            pltpu.VMEM((SEQ_TILE, w), BF16),
            pltpu.VMEM((SEQ_TILE, 2 * w), F32),
            pltpu.VMEM((SEQ_TILE, 2 * w), BF16),
            pltpu.VMEM((3 * SUBLANES, zin), F32),
            pltpu.VMEM((SUBLANES, w), F32),
        ],
        compiler_params=_cparams(),
        name="lru_sconv_mixer",
    )(x, *params)


def _pair_levels():
    p = np.arange(PT)
    t = (p % SUBLANES) * SEG + p // SUBLANES
    x = t[:, None] ^ t[None, :]
    lvl = np.full(x.shape, -1.0, np.float32)
    for b in range(TIME_BITS):
        lvl[(x >> b) == 1] = b
    return lvl


def _sublane_row(v, s):
    return jnp.broadcast_to(v[s:s + 1, :], v.shape)


def _odd_kernel(x_ref, g_ref, win_ref, pw_ref, pscale_ref, lbl_ref, ng_ref, lvl_ref, wout_ref, o_ref,
                h_ref, z_ref, pa_ref, pb_ref, pcarry_ref, cum_ref, k_ref, lvq_ref, lvk_ref, qh_ref, kh_ref,
                vb_ref, oh_ref, y_ref, state_ref, *, layer):
    w = pscale_ref.shape[1]
    heads = w // LANES
    j = pl.program_id(1)
    hist = POOL_WINDOWS[-1] // 2
    hrows = hist * SUBLANES

    @pl.when(jnp.logical_and(pl.program_id(0) == 0, j == 0))
    def _():
        lvq_ref[...] = jnp.zeros_like(lvq_ref)
        lvk_ref[...] = jnp.zeros_like(lvk_ref)

    @pl.when(j == 0)
    def _():
        pcarry_ref[...] = jnp.zeros_like(pcarry_ref)
        state_ref[...] = jnp.zeros_like(state_ref)

    uc_c, q_c, fz_c, v_c, gd_c = (k * w for k in range(5))

    logits = lbl_ref[...]
    sm = jnp.exp(logits - jnp.max(logits, axis=0, keepdims=True))
    sm = sm / jnp.sum(sm, axis=0, keepdims=True)
    lb = jnp.zeros((1, w), F32)
    for o in range(1, layer + 1):
        lb = lb + sm[o:o + 1, :]
    log_1m_lb = jnp.log1p(-lb)

    sub8 = lax.broadcasted_iota(jnp.int32, (SUBLANES, LANES), 0)
    row16 = lax.broadcasted_iota(jnp.int32, (PAIR_ROWS, LANES), 0)
    first_masked = jnp.where(row16 < SUBLANES, MASKED_OUT, 0.0)
    nt = (((1,), (1,)), ((), ()))
    tn = (((0,), (0,)), ((), ()))
    twice = lambda v: jnp.concatenate([v, v], axis=0)
    bufs = (pa_ref, pb_ref)

    blk_rows = lambda st: pl.ds(st * PT, PT)

    def pool(st):
        base = st * PT
        pa_ref[pl.ds(hrows, PT), :] = z_ref[blk_rows(st), pl.ds(uc_c, w)]
        for stage in range(len(POOL_WINDOWS)):
            shift = 1 << stage
            src = bufs[stage % 2]
            dst = bufs[(stage + 1) % 2]
            lanes = pl.ds(stage * LANES, w - stage * LANES)
            for gidx in range(shift):
                tail = src[pl.ds(hrows + PT - (shift - gidx) * SUBLANES, SUBLANES), lanes]
                rolled = pltpu.roll(tail, 1, axis=0)
                crow = pl.ds((shift - 1 + gidx) * SUBLANES, SUBLANES)
                prev = pcarry_ref[crow, lanes]
                src[pl.ds(hrows - (shift - gidx) * SUBLANES, SUBLANES), lanes] = jnp.where(
                    lax.broadcasted_iota(jnp.int32, rolled.shape, 0) == 0, prev, rolled)
                pcarry_ref[crow, lanes] = rolled
            for r in range(CHUNKS_PER_BLOCK):
                r0 = r * ROW_CHUNK + hrows
                dst[pl.ds(r0, ROW_CHUNK), lanes] = (
                    src[pl.ds(r0, ROW_CHUNK), lanes] + src[pl.ds(r0 - shift * SUBLANES, ROW_CHUNK), lanes])

        for r in range(CHUNKS_PER_BLOCK):
            t = _row_time(j * SEQ_TILE + base + r * (ROW_CHUNK // SUBLANES), (ROW_CHUNK, LANES))
            for gi, win in enumerate(POOL_WINDOWS):
                buf = bufs[(gi + 1) % 2]
                lanes = pl.ds(gi * LANES, LANES)
                cnt = jnp.minimum(t + 1, win).astype(F32)
                rows = pl.ds(base + r * ROW_CHUNK, ROW_CHUNK)
                p = buf[pl.ds(hrows + r * ROW_CHUNK, ROW_CHUNK), lanes] / cnt - z_ref[rows, pl.ds(uc_c + gi * LANES, LANES)]
                y_ref[rows, lanes] = p.astype(BF16)

        for blk in range(w // MXU_DIM):
            lanes = pl.ds(blk * MXU_DIM, MXU_DIM)
            mixed = jnp.dot(y_ref[blk_rows(st), lanes], pw_ref[blk], preferred_element_type=F32)
            y_ref[blk_rows(st), lanes] = (mixed * pscale_ref[:, lanes]).astype(BF16)

    totals = {}

    def head_factors(st, hd):
        base = st * PT
        par = hd % 2
        cum_b, k_b, lvq, lvk = cum_ref.at[par], k_ref.at[par], lvq_ref.at[par], lvk_ref.at[par]
        qh_b, kh_b, vb_b = qh_ref.at[par], kh_ref.at[par], vb_ref.at[par]
        lbh = lb[:, hd * LANES:(hd + 1) * LANES]
        l1 = log_1m_lb[:, hd * LANES:(hd + 1) * LANES]
        om = 1.0 - lbh
        q_l = pl.ds(q_c + hd * LANES, LANES)
        fz_l = pl.ds(fz_c + hd * LANES, LANES)
        v_l = pl.ds(v_c + hd * LANES, LANES)

        cum = jnp.zeros((SUBLANES, LANES), F32)
        for r in range(CHUNKS_PER_BLOCK):
            fz = z_ref[pl.ds(base + r * ROW_CHUNK, ROW_CHUNK), fz_l]
            e = jnp.exp(-jnp.abs(fz))
            rcp = 1.0 / (1.0 + e)
            er = e * rcp
            pos = fz >= 0
            f = lbh + om * jnp.where(pos, rcp, er)
            log_f = LOG2_E * jnp.maximum(jnp.log(f), l1 + (jnp.minimum(fz, 0.0) - e))
            k_b[pl.ds(r * ROW_CHUNK, ROW_CHUNK), :] = om * jnp.where(pos, er, rcp)
            parts = []
            for g in range(ROW_CHUNK // SUBLANES):
                cum = cum + log_f[g * SUBLANES:(g + 1) * SUBLANES, :]
                parts.append(cum)
            cum_b[pl.ds(r * ROW_CHUNK, ROW_CHUNK), :] = jnp.concatenate(parts, axis=0)
        cum_last = cum

        incl = cum_last
        for sh in (1, 2, 4):
            incl = incl + jnp.where(sub8 >= sh, pltpu.roll(incl, sh, axis=0), 0.0)
        offset = incl - cum_last
        total = _sublane_row(incl, SUBLANES - 1)
        totals[st, hd] = total

        seg_levels = []
        for c in range(TIME_BITS - SEG_BITS):
            upper8 = ((sub8 >> c) & 1) == 1
            incl_mid = jnp.zeros_like(incl)
            for s_mid in sorted({((s >> (c + 1)) << (c + 1)) | ((1 << c) - 1) for s in range(SUBLANES)}):
                in_block = (sub8 >> (c + 1)) == (s_mid >> (c + 1))
                incl_mid = jnp.where(in_block, _sublane_row(incl, s_mid), incl_mid)
            q_add = jnp.where(upper8, offset - incl_mid, MASKED_OUT)
            k_add = jnp.where(upper8, MASKED_OUT, incl_mid - incl + cum_last)
            seg_levels.append((twice(q_add), twice(k_add)))
        q_const = twice(offset)
        k_const = twice(cum_last + total - incl)

        for m in range(SEG // 2):
            rows = pl.ds(m * PAIR_ROWS, PAIR_ROWS)
            q = z_ref[pl.ds(base + m * PAIR_ROWS, PAIR_ROWS), q_l]
            k = k_b[rows, :]
            cum = cum_b[rows, :]
            for b in range(SEG_BITS):
                if b == 0:
                    dec = jnp.exp2(cum - twice(cum[0:SUBLANES, :]) + first_masked)
                    lvq[0, rows, :] = (q * dec).astype(BF16)
                    lvk[0, rows, :] = jnp.where(row16 < SUBLANES, k, 0.0).astype(BF16)
                    continue
                mid = ((m >> b) << (b + 1)) | ((1 << b) - 1)
                ref = twice(cum_b[pl.ds(mid * SUBLANES, SUBLANES), :])
                if (m >> (b - 1)) & 1:
                    lvq[b, rows, :] = (q * jnp.exp2(cum - ref)).astype(BF16)
                else:
                    lvk[b, rows, :] = (k * jnp.exp2(ref - cum)).astype(BF16)
            ncum = -cum
            for b in range(SEG_BITS, TIME_BITS):
                q_add, k_add = seg_levels[b - SEG_BITS]
                lvq[b, rows, :] = (q * jnp.exp2(cum + q_add)).astype(BF16)
                lvk[b, rows, :] = (k * jnp.exp2(ncum + k_add)).astype(BF16)
            qh_b[rows, :] = (q * jnp.exp2(cum + q_const)).astype(BF16)
            kh_b[rows, :] = (k * jnp.exp2(ncum + k_const)).astype(BF16)
        vb_b[...] = z_ref[blk_rows(st), v_l].astype(BF16)

    def head_products(st, hd):
        par = hd % 2
        k_b, lvq, lvk = k_ref.at[par], lvq_ref.at[par], lvk_ref.at[par]
        qh_b, kh_b, vb_b = qh_ref.at[par], kh_ref.at[par], vb_ref.at[par]
        q_l = pl.ds(q_c + hd * LANES, LANES)
        v_l = pl.ds(v_c + hd * LANES, LANES)
        att = jnp.zeros((PT, PT), BF16)
        for b in range(TIME_BITS):
            part = lax.dot_general(lvq[b], lvk[b], nt, preferred_element_type=F32)
            att = jnp.where(lvl_ref[...] == b, part.astype(BF16), att)

        state = state_ref[hd]
        diag = jnp.sum(z_ref[blk_rows(st), q_l] * k_b[...], axis=-1, keepdims=True)
        o = jnp.dot(att, vb_b[...], preferred_element_type=F32)
        o = o + lax.dot_general(qh_b[...], state.astype(BF16), nt, preferred_element_type=F32)
        oh_ref[hd, blk_rows(st), :] = o + diag * z_ref[blk_rows(st), v_l]
        upd = lax.dot_general(vb_b[...], kh_b[...], tn, preferred_element_type=F32)
        state_ref[hd] = state * jnp.exp2(totals[st, hd][0:1, :]) + upd

    def head_outputs(st, r):
        rows = pl.ds(st * PT + r * ROW_CHUNK, ROW_CHUNK)
        for hd in range(heads):
            o = oh_ref[hd, rows, :]
            o = o * lax.rsqrt(jnp.mean(o * o, axis=-1, keepdims=True) + EPS) * ng_ref[:, pl.ds(hd * LANES, LANES)]
            gd = z_ref[rows, pl.ds(gd_c + hd * LANES, LANES)]
            y_ref[rows, pl.ds(w + hd * LANES, LANES)] = (o * (gd * _sigmoid(gd))).astype(BF16)

    _rmsnorm_to(x_ref, g_ref, h_ref)
    z_ref[...] = jnp.dot(h_ref[...], win_ref[...], preferred_element_type=F32)
    for st in range(NSUB):
        pool(st)
        for hd in range(heads):
            head_factors(st, hd)
            head_products(st, hd)
        for r in range(CHUNKS_PER_BLOCK):
            head_outputs(st, r)
    o_ref[...] = x_ref[...] + jnp.dot(y_ref[...], wout_ref[...], preferred_element_type=F32)


def _odd_call(x, params, layer):
    bsz, seq, d = x.shape
    g, win, pw, pscale, lbl, ng, levels, wout = params
    zin = win.shape[-1]
    w = zin // 5
    heads = w // LANES
    hist = POOL_WINDOWS[-1] // 2
    o = layer // 2
    xspec = pl.BlockSpec((None, SEQ_TILE, d), lambda b, j: (b, j, 0))
    specs = [_layer_spec(g, layer), _layer_spec(win, o), _layer_spec(pw, o), _layer_spec(pscale, o),
             _whole_spec(lbl), _layer_spec(ng, o), _whole_spec(levels), _layer_spec(wout, o)]
    return pl.pallas_call(
        functools.partial(_odd_kernel, layer=o),
        out_shape=jax.ShapeDtypeStruct(x.shape, x.dtype),
        grid=(bsz, seq // SEQ_TILE),
        in_specs=[xspec] + specs,
        out_specs=xspec,
        scratch_shapes=[
            pltpu.VMEM((SEQ_TILE, d), BF16),
            pltpu.VMEM((SEQ_TILE, zin), F32),
            pltpu.VMEM((PT + hist * SUBLANES, w), F32),
            pltpu.VMEM((PT + hist * SUBLANES, w), F32),
            pltpu.VMEM(((POOL_WINDOWS[-1] - 1) * SUBLANES, w), F32),
            pltpu.VMEM((2, PT, LANES), F32),
            pltpu.VMEM((2, PT, LANES), F32),
            pltpu.VMEM((2, TIME_BITS, PT, LANES), BF16),
            pltpu.VMEM((2, TIME_BITS, PT, LANES), BF16),
            pltpu.VMEM((2, PT, LANES), BF16),
            pltpu.VMEM((2, PT, LANES), BF16),
            pltpu.VMEM((2, PT, LANES), BF16),
            pltpu.VMEM((heads, SEQ_TILE, LANES), F32),
            pltpu.VMEM((SEQ_TILE, 2 * w), BF16),
            pltpu.VMEM((heads, LANES, LANES), F32),
        ],
        compiler_params=_cparams(),
        name="pool_hgrn_mixer",
    )(x, *params)


def _cast_kernel(x_ref, o_ref):
    o_ref[...] = x_ref[...].astype(o_ref.dtype)


def _to_bf16(wstack):
    rows = int(np.prod(wstack.shape[:-1]))
    cols = wstack.shape[-1]
    assert rows % CAST_ROWS == 0
    spec = pl.BlockSpec((CAST_ROWS, cols), lambda i: (i, 0))
    out = pl.pallas_call(
        _cast_kernel,
        out_shape=jax.ShapeDtypeStruct((rows, cols), BF16),
        grid=(rows // CAST_ROWS,),
        in_specs=[spec],
        out_specs=spec,
        name="weights_to_bf16",
    )(wstack.reshape(rows, cols))
    return out.reshape(wstack.shape)


def _to_segment_major(x):
    b, s, d = x.shape
    return x.reshape(b, s // PT, SUBLANES, SEG, d).transpose(0, 1, 3, 2, 4).reshape(b, s, d)


def _from_segment_major(x):
    b, s, d = x.shape
    return x.reshape(b, s // PT, SEG, SUBLANES, d).transpose(0, 1, 3, 2, 4).reshape(b, s, d)


def _block_diag_tiles(wblocks):
    layers, nb, n, _ = wblocks.shape
    per = MXU_DIM // n
    grouped = wblocks.reshape(layers, nb // per, per, n, n)
    tiles = jnp.einsum("ab,ltaij->ltaibj", jnp.eye(per, dtype=wblocks.dtype), grouped)
    return tiles.reshape(layers, nb // per, MXU_DIM, MXU_DIM).astype(BF16)


def kernel(x, g_mix, g_ffn, g_final, w_in_even, w_out_even, lru_conv_w, lru_conv_b, lru_wa, lru_ba, lru_wi, lru_bi, lru_lambda, sconv_w, w_in_odd, w_out_odd, pool_w, pool_scale, hgrn_lb_logits, hgrn_norm_g, ffn_w_up, ffn_w_gate, ffn_conv_w, ffn_conv_b, ffn_w_down):
    depth = g_mix.shape[0]
    assert x.shape[1] % SEQ_TILE == 0
    rows = lambda v: v.reshape(v.shape[0], 1, -1).astype(F32)
    levels = jnp.asarray(_pair_levels(), dtype=BF16)
    even = (rows(g_mix), _to_bf16(w_in_even), lru_conv_w, rows(lru_conv_b), _block_diag_tiles(lru_wa),
            _block_diag_tiles(lru_wi), rows(lru_ba), rows(lru_bi), rows(lru_lambda), sconv_w, _to_bf16(w_out_even))
    odd = (rows(g_mix), _to_bf16(w_in_odd), _block_diag_tiles(pool_w), rows(pool_scale),
           hgrn_lb_logits.astype(F32), rows(hgrn_norm_g), levels, _to_bf16(w_out_odd))
    ffn = (rows(g_ffn), _to_bf16(ffn_w_up), _to_bf16(ffn_w_gate), ffn_conv_w, rows(ffn_conv_b),
           _to_bf16(ffn_w_down), g_final.reshape(1, -1).astype(F32))
    x = _to_segment_major(x)
    for l in range(depth):
        if l % 2 == 0:
            x = _even_call(x, even, l)
        else:
            x = _odd_call(x, odd, l)
        x = _ffn_call(x, ffn, l, l == depth - 1)
    return _from_segment_major(x)
```

```python
import functools

import jax
import jax.numpy as jnp
import numpy as np
from jax import lax
from jax.experimental import pallas as pl
from jax.experimental.pallas import tpu as pltpu

F32 = jnp.float32
BF16 = jnp.bfloat16

SUBLANES = 8
LANES = 128
MXU_DIM = 256
VMEM_LIMIT_BYTES = 56 * 1024 * 1024

EPS = 1e-6
LOG2_E = 1.4426950408889634
MASKED_OUT = -1e30
LRU_C = 8.0
POOL_WINDOWS = (2, 4, 8, 16)

SEQ_TILE = 512
PT = 256
NSUB = SEQ_TILE // PT
SEG = PT // SUBLANES
SEG_BITS = SEG.bit_length() - 1
TIME_BITS = PT.bit_length() - 1
ROW_CHUNK = 64
CHUNKS_PER_BLOCK = PT // ROW_CHUNK
FFN_CHUNK = 256
PAIR_ROWS = 2 * SUBLANES
CAST_ROWS = 512


def _cparams():
    return pltpu.CompilerParams(
        dimension_semantics=("arbitrary", "arbitrary"),
        vmem_limit_bytes=VMEM_LIMIT_BYTES,
    )


def _sigmoid(v):
    return 0.5 * jnp.tanh(0.5 * v) + 0.5


def _rmsnorm_rows(x, g):
    ms = jnp.mean(x * x, axis=-1, keepdims=True)
    return x * lax.rsqrt(ms + EPS) * g


def _rmsnorm_to(x_ref, g_ref, h_ref, row0=0, nrows=SEQ_TILE):
    for r in range(nrows // ROW_CHUNK):
        rows = pl.ds(row0 + r * ROW_CHUNK, ROW_CHUNK)
        h_ref[rows, :] = _rmsnorm_rows(x_ref[rows, :], g_ref[...]).astype(h_ref.dtype)


def _fill_history(buf_ref, base, carry_ref, groups, lanes, carry_lanes=None):
    carry_lanes = lanes if carry_lanes is None else carry_lanes
    for g in range(groups):
        tail = buf_ref[pl.ds(base + PT + g * SUBLANES, SUBLANES), lanes]
        rolled = pltpu.roll(tail, 1, axis=0)
        sub = lax.broadcasted_iota(jnp.int32, rolled.shape, 0)
        prev = carry_ref[pl.ds(g * SUBLANES, SUBLANES), carry_lanes]
        buf_ref[pl.ds(base + g * SUBLANES, SUBLANES), lanes] = jnp.where(sub == 0, prev, rolled)
        carry_ref[pl.ds(g * SUBLANES, SUBLANES), carry_lanes] = rolled


def _row_time(t0, shape):
    row = lax.broadcasted_iota(jnp.int32, shape, 0)
    sub = row & (SUBLANES - 1)
    grp = row >> (SUBLANES.bit_length() - 1)
    return t0 + sub * SEG + grp


def _layer_spec(a, l):
    return pl.BlockSpec((None,) + a.shape[1:], lambda b, j: (l,) + (0,) * (a.ndim - 1))


def _whole_spec(a):
    return pl.BlockSpec(a.shape, lambda b, j: (0,) * a.ndim)


def _ffn_kernel(x_ref, g_ref, wup_ref, wgate_ref, cw_ref, cb_ref, wdown_ref, gfin_ref, o_ref,
                h_ref, up_ref, gate_ref, act_ref, carry_ref, *, final_norm):
    hist = 2
    hrows = hist * SUBLANES
    stride = hrows + PT
    ffn_dim = wup_ref.shape[1]

    @pl.when(pl.program_id(1) == 0)
    def _():
        carry_ref[...] = jnp.zeros_like(carry_ref)

    _rmsnorm_to(x_ref, g_ref, h_ref)

    for c in range(ffn_dim // FFN_CHUNK):
        cols = pl.ds(c * FFN_CHUNK, FFN_CHUNK)
        up = up_ref.at[c % 2]
        gate = gate_ref.at[c % 2]
        proj = jnp.dot(h_ref[...], wup_ref[:, cols], preferred_element_type=F32)
        for st in range(NSUB):
            up[pl.ds(st * stride + hrows, PT), :] = proj[st * PT:(st + 1) * PT, :]
        gate[...] = jnp.dot(h_ref[...], wgate_ref[:, cols], preferred_element_type=F32)
        w0 = cw_ref[0:1, cols]
        w1 = cw_ref[1:2, cols]
        w2 = cw_ref[2:3, cols]
        b = cb_ref[:, cols]
        for st in range(NSUB):
            _fill_history(up, st * stride, carry_ref, hist, slice(None), cols)
            for r in range(CHUNKS_PER_BLOCK):
                r0 = st * stride + r * ROW_CHUNK
                u = (up[pl.ds(r0 + hrows, ROW_CHUNK), :] * w2
                     + up[pl.ds(r0 + hrows - SUBLANES, ROW_CHUNK), :] * w1
                     + up[pl.ds(r0, ROW_CHUNK), :] * w0 + b)
                rows = pl.ds(st * PT + r * ROW_CHUNK, ROW_CHUNK)
                act_ref[rows, cols] = (jax.nn.gelu(u) * gate[rows, :]).astype(BF16)

    o_ref[...] = x_ref[...] + jnp.dot(act_ref[...], wdown_ref[...], preferred_element_type=F32)
    if final_norm:
        for r in range(SEQ_TILE // ROW_CHUNK):
            rows = pl.ds(r * ROW_CHUNK, ROW_CHUNK)
            o_ref[rows, :] = _rmsnorm_rows(o_ref[rows, :], gfin_ref[...])


def _ffn_call(x, params, layer, final_norm):
    bsz, seq, d = x.shape
    g, wup, wgate, cw, cb, wdown, gfin = params
    f = wup.shape[-1]
    xspec = pl.BlockSpec((None, SEQ_TILE, d), lambda b, j: (b, j, 0))
    return pl.pallas_call(
        functools.partial(_ffn_kernel, final_norm=final_norm),
        out_shape=jax.ShapeDtypeStruct(x.shape, x.dtype),
        grid=(bsz, seq // SEQ_TILE),
        in_specs=[xspec] + [_layer_spec(a, layer) for a in params[:-1]] + [_whole_spec(gfin)],
        out_specs=xspec,
        scratch_shapes=[
            pltpu.VMEM((SEQ_TILE, d), BF16),
            pltpu.VMEM((2, NSUB * (PT + 2 * SUBLANES), FFN_CHUNK), F32),
            pltpu.VMEM((2, SEQ_TILE, FFN_CHUNK), F32),
            pltpu.VMEM((SEQ_TILE, f), BF16),
            pltpu.VMEM((2 * SUBLANES, f), F32),
        ],
        compiler_params=_cparams(),
        name="conv_ffn",
    )(x, *params)


def _even_kernel(x_ref, g_ref, win_ref, cw_ref, cb_ref, wa_ref, wi_ref, ba_ref, bi_ref, lam_ref,
                 sw_ref, wout_ref, o_ref,
                 h_ref, *scratch):
    z_refs = scratch[:NSUB]
    xc_ref, xcb_ref, gate_ref, y_ref, zcarry_ref, hcarry_ref = scratch[NSUB:]
    hist = 3
    hrows = hist * SUBLANES
    w = xc_ref.shape[1]
    j = pl.program_id(1)

    @pl.when(j == 0)
    def _():
        zcarry_ref[...] = jnp.zeros_like(zcarry_ref)
        hcarry_ref[...] = jnp.zeros_like(hcarry_ref)

    xa_c, ga_c, hb_c, bg_c, cg_c = (pl.ds(k * w, w) for k in range(5))
    r_l, i_l = pl.ds(0, w), pl.ds(w, w)
    sub = lax.broadcasted_iota(jnp.int32, (SUBLANES, w), 0)

    lam = lam_ref[...]
    neg_c_sp = (-LRU_C * LOG2_E) * (jnp.maximum(-lam, 0.0) + jnp.log1p(jnp.exp(-jnp.abs(lam))))

    blk_rows = lambda st: pl.ds(st * PT, PT)
    chunk_rows = lambda st, r: pl.ds(st * PT + r * ROW_CHUNK, ROW_CHUNK)
    h_rep = {}

    def project(st, k):
        cols = pl.ds(k * w, w)
        z_refs[st][pl.ds(hrows, PT), cols] = jnp.dot(
            h_ref[blk_rows(st), :], win_ref[:, cols], preferred_element_type=F32)

    def history(st):
        for lanes in (xa_c, hb_c, cg_c):
            _fill_history(z_refs[st], 0, zcarry_ref, hist, lanes)

    def lru_conv(st, r):
        z_ref = z_refs[st]
        z0 = hrows + r * ROW_CHUNK
        acc = cb_ref[...] + z_ref[pl.ds(z0, ROW_CHUNK), xa_c] * cw_ref[3:4, :]
        for k in range(1, 4):
            acc = acc + z_ref[pl.ds(z0 - k * SUBLANES, ROW_CHUNK), xa_c] * cw_ref[3 - k:4 - k, :]
        xc_ref[chunk_rows(st, r), :] = acc
        xcb_ref[chunk_rows(st, r), :] = acc.astype(BF16)

    def gate_matmuls(st):
        for blk in range(w // MXU_DIM):
            lanes = pl.ds(blk * MXU_DIM, MXU_DIM)
            gate_ref[blk_rows(st), pl.ds(blk * MXU_DIM, MXU_DIM)] = jnp.dot(
                xcb_ref[blk_rows(st), lanes], wa_ref[blk], preferred_element_type=F32)
            gate_ref[blk_rows(st), pl.ds(w + blk * MXU_DIM, MXU_DIM)] = jnp.dot(
                xcb_ref[blk_rows(st), lanes], wi_ref[blk], preferred_element_type=F32)

    def gates(st, r):
        rows = chunk_rows(st, r)
        rg = _sigmoid(gate_ref[rows, r_l] + ba_ref[...])
        ig = _sigmoid(gate_ref[rows, i_l] + bi_ref[...])
        a = jnp.exp2(rg * neg_c_sp)
        mult = jnp.sqrt(1.0 - a * a)
        if st == 0 and r == 0:
            t = _row_time(j * SEQ_TILE, mult.shape)
            mult = jnp.where(t == 0, 1.0, mult)
        xc_ref[rows, :] = mult * ig * xc_ref[rows, :]
        gate_ref[rows, r_l] = a

    def scan(st):
        h = jnp.zeros((SUBLANES, w), F32)
        p = h + 1.0
        for i in range(SEG):
            rows = pl.ds(st * PT + i * SUBLANES, SUBLANES)
            a = gate_ref[rows, r_l]
            h = a * h + xc_ref[rows, :]
            p = a * p
            xc_ref[rows, :] = h
            gate_ref[rows, r_l] = p
        h_in = jnp.where(sub == 0, hcarry_ref[...], 0.0)
        for s in range(SUBLANES - 1):
            nxt = pltpu.roll(h + p * h_in, 1, axis=0)
            h_in = jnp.where(sub == s + 1, nxt, h_in)
        hcarry_ref[...] = pltpu.roll(h + p * h_in, 1, axis=0)
        h_rep[st] = jnp.concatenate([h_in] * (ROW_CHUNK // SUBLANES), axis=0)

    def outputs(st, r):
        z_ref = z_refs[st]
        rows = chunk_rows(st, r)
        z0 = hrows + r * ROW_CHUNK
        hs = xc_ref[rows, :] + gate_ref[rows, r_l] * h_rep[st]
        ya = hs * jax.nn.gelu(z_ref[pl.ds(z0, ROW_CHUNK), ga_c])
        conv = None
        for k in range(3):
            zrows = pl.ds(z0 - k * SUBLANES, ROW_CHUNK)
            term = z_ref[zrows, cg_c] * z_ref[zrows, hb_c] * sw_ref[2 - k:3 - k, :]
            conv = term if conv is None else conv + term
        yb = z_ref[pl.ds(z0, ROW_CHUNK), bg_c] * conv
        y_ref[rows, pl.ds(0, w)] = ya.astype(BF16)
        y_ref[rows, pl.ds(w, w)] = yb.astype(BF16)

    def out_project(st, n):
        cols = pl.ds(n * MXU_DIM, MXU_DIM)
        o_ref[blk_rows(st), cols] = x_ref[blk_rows(st), cols] + jnp.dot(
            y_ref[blk_rows(st), :], wout_ref[:, cols], preferred_element_type=F32)

    d_model = x_ref.shape[1]
    n_out = d_model // MXU_DIM
    _rmsnorm_to(x_ref, g_ref, h_ref, 0, PT)
    for k in range(5):
        project(0, k)
    for st in range(NSUB):
        nxt = st + 1 if st + 1 < NSUB else None
        prv = st - 1 if st > 0 else None
        pieces = []
        if nxt is not None:
            pieces += [functools.partial(project, nxt, k) for k in range(5)]
        if prv is not None:
            pieces += [functools.partial(out_project, prv, n) for n in range(n_out)]
        vector_steps = ([functools.partial(lru_conv, st, r) for r in range(CHUNKS_PER_BLOCK)]
                        + [functools.partial(gate_matmuls, st)]
                        + [functools.partial(gates, st, r) for r in range(CHUNKS_PER_BLOCK)]
                        + [functools.partial(scan, st)]
                        + [functools.partial(outputs, st, r) for r in range(CHUNKS_PER_BLOCK)])
        history(st)
        if nxt is not None:
            _rmsnorm_to(x_ref, g_ref, h_ref, nxt * PT, PT)
        emitted = 0
        for idx, step in enumerate(vector_steps):
            step()
            due = (idx + 1) * len(pieces) // len(vector_steps)
            while emitted < due:
                pieces[emitted]()
                emitted += 1
    for n in range(n_out):
        out_project(NSUB - 1, n)


def _even_call(x, params, layer):
    bsz, seq, d = x.shape
    zin = params[1].shape[-1]
    w = zin // 5
    xspec = pl.BlockSpec((None, SEQ_TILE, d), lambda b, j: (b, j, 0))
    specs = [_layer_spec(params[0], layer)] + [_layer_spec(a, layer // 2) for a in params[1:]]
    return pl.pallas_call(
        _even_kernel,
        out_shape=jax.ShapeDtypeStruct(x.shape, x.dtype),
        grid=(bsz, seq // SEQ_TILE),
        in_specs=[xspec] + specs,
        out_specs=xspec,
        scratch_shapes=[
            pltpu.VMEM((SEQ_TILE, d), BF16),
        ] + [pltpu.VMEM((PT + 3 * SUBLANES, zin), F32) for _ in range(NSUB)] + [
            pltpu.VMEM((SEQ_TILE, w), F32),
            pltpu.VMEM((SEQ_TILE, w), BF16),
            pltpu.VMEM((SEQ_TILE, 2 * w), F32),
            pltpu.VMEM((SEQ_TILE, 2 * w), BF16),
            pltpu.VMEM((3 * SUBLANES, zin), F32),
            pltpu.VMEM((SUBLANES, w), F32),
        ],
        compiler_params=_cparams(),
        name="lru_sconv_mixer",
    )(x, *params)


def _pair_levels():
    p = np.arange(PT)
    t = (p % SUBLANES) * SEG + p // SUBLANES
    x = t[:, None] ^ t[None, :]
    lvl = np.full(x.shape, -1.0, np.float32)
    for b in range(TIME_BITS):
        lvl[(x >> b) == 1] = b
    return lvl


def _sublane_row(v, s):
    return jnp.broadcast_to(v[s:s + 1, :], v.shape)


def _odd_kernel(x_ref, g_ref, win_ref, pw_ref, pscale_ref, lbl_ref, ng_ref, lvl_ref, wout_ref, o_ref,
                h_ref, z_ref, pa_ref, pb_ref, pcarry_ref, cum_ref, k_ref, lvq_ref, lvk_ref, qh_ref, kh_ref,
                vb_ref, oh_ref, y_ref, state_ref, *, layer):
    w = pscale_ref.shape[1]
    heads = w // LANES
    j = pl.program_id(1)
    hist = POOL_WINDOWS[-1] // 2
    hrows = hist * SUBLANES

    @pl.when(jnp.logical_and(pl.program_id(0) == 0, j == 0))
    def _():
        lvq_ref[...] = jnp.zeros_like(lvq_ref)
        lvk_ref[...] = jnp.zeros_like(lvk_ref)

    @pl.when(j == 0)
    def _():
        pcarry_ref[...] = jnp.zeros_like(pcarry_ref)
        state_ref[...] = jnp.zeros_like(state_ref)

    uc_c, q_c, fz_c, v_c, gd_c = (k * w for k in range(5))

    logits = lbl_ref[...]
    sm = jnp.exp(logits - jnp.max(logits, axis=0, keepdims=True))
    sm = sm / jnp.sum(sm, axis=0, keepdims=True)
    lb = jnp.zeros((1, w), F32)
    for o in range(1, layer + 1):
        lb = lb + sm[o:o + 1, :]
    log_1m_lb = jnp.log1p(-lb)

    sub8 = lax.broadcasted_iota(jnp.int32, (SUBLANES, LANES), 0)
    row16 = lax.broadcasted_iota(jnp.int32, (PAIR_ROWS, LANES), 0)
    first_masked = jnp.where(row16 < SUBLANES, MASKED_OUT, 0.0)
    nt = (((1,), (1,)), ((), ()))
    tn = (((0,), (0,)), ((), ()))
    twice = lambda v: jnp.concatenate([v, v], axis=0)
    bufs = (pa_ref, pb_ref)

    blk_rows = lambda st: pl.ds(st * PT, PT)

    def pool(st):
        base = st * PT
        pa_ref[pl.ds(hrows, PT), :] = z_ref[blk_rows(st), pl.ds(uc_c, w)]
        for stage in range(len(POOL_WINDOWS)):
            shift = 1 << stage
            src = bufs[stage % 2]
            dst = bufs[(stage + 1) % 2]
            lanes = pl.ds(stage * LANES, w - stage * LANES)
            for gidx in range(shift):
                tail = src[pl.ds(hrows + PT - (shift - gidx) * SUBLANES, SUBLANES), lanes]
                rolled = pltpu.roll(tail, 1, axis=0)
                crow = pl.ds((shift - 1 + gidx) * SUBLANES, SUBLANES)
                prev = pcarry_ref[crow, lanes]
                src[pl.ds(hrows - (shift - gidx) * SUBLANES, SUBLANES), lanes] = jnp.where(
                    lax.broadcasted_iota(jnp.int32, rolled.shape, 0) == 0, prev, rolled)
                pcarry_ref[crow, lanes] = rolled
            for r in range(CHUNKS_PER_BLOCK):
                r0 = r * ROW_CHUNK + hrows
                dst[pl.ds(r0, ROW_CHUNK), lanes] = (
                    src[pl.ds(r0, ROW_CHUNK), lanes] + src[pl.ds(r0 - shift * SUBLANES, ROW_CHUNK), lanes])

        for r in range(CHUNKS_PER_BLOCK):
            short = st == 0 and r * (ROW_CHUNK // SUBLANES) < POOL_WINDOWS[-1]
            t = _row_time(j * SEQ_TILE + r * (ROW_CHUNK // SUBLANES), (ROW_CHUNK, LANES)) if short else None
            for gi, win in enumerate(POOL_WINDOWS):
                buf = bufs[(gi + 1) % 2]
                lanes = pl.ds(gi * LANES, LANES)
                rows = pl.ds(base + r * ROW_CHUNK, ROW_CHUNK)
                total = buf[pl.ds(hrows + r * ROW_CHUNK, ROW_CHUNK), lanes]
                mean = total / jnp.minimum(t + 1, win).astype(F32) if short else total * (1.0 / win)
                y_ref[rows, lanes] = (mean - z_ref[rows, pl.ds(uc_c + gi * LANES, LANES)]).astype(BF16)

        for blk in range(w // MXU_DIM):
            lanes = pl.ds(blk * MXU_DIM, MXU_DIM)
            mixed = jnp.dot(y_ref[blk_rows(st), lanes], pw_ref[blk], preferred_element_type=F32)
            y_ref[blk_rows(st), lanes] = (mixed * pscale_ref[:, lanes]).astype(BF16)

    totals = {}

    def head_factors(st, hd):
        base = st * PT
        par = hd % 2
        cum_b, k_b, lvq, lvk = cum_ref.at[par], k_ref.at[par], lvq_ref.at[par], lvk_ref.at[par]
        qh_b, kh_b, vb_b = qh_ref.at[par], kh_ref.at[par], vb_ref.at[par]
        lbh = lb[:, hd * LANES:(hd + 1) * LANES]
        l1 = log_1m_lb[:, hd * LANES:(hd + 1) * LANES]
        om = 1.0 - lbh
        q_l = pl.ds(q_c + hd * LANES, LANES)
        fz_l = pl.ds(fz_c + hd * LANES, LANES)
        v_l = pl.ds(v_c + hd * LANES, LANES)

        cum = jnp.zeros((SUBLANES, LANES), F32)
        for r in range(CHUNKS_PER_BLOCK):
            fz = z_ref[pl.ds(base + r * ROW_CHUNK, ROW_CHUNK), fz_l]
            e = jnp.exp(-jnp.abs(fz))
            rcp = 1.0 / (1.0 + e)
            er = e * rcp
            pos = fz >= 0
            f = lbh + om * jnp.where(pos, rcp, er)
            log_f = LOG2_E * jnp.maximum(jnp.log(f), l1 + (jnp.minimum(fz, 0.0) - e))
            k_b[pl.ds(r * ROW_CHUNK, ROW_CHUNK), :] = om * jnp.where(pos, er, rcp)
            parts = []
            for g in range(ROW_CHUNK // SUBLANES):
                cum = cum + log_f[g * SUBLANES:(g + 1) * SUBLANES, :]
                parts.append(cum)
            cum_b[pl.ds(r * ROW_CHUNK, ROW_CHUNK), :] = jnp.concatenate(parts, axis=0)
        cum_last = cum

        incl = cum_last
        for sh in (1, 2, 4):
            incl = incl + jnp.where(sub8 >= sh, pltpu.roll(incl, sh, axis=0), 0.0)
        offset = incl - cum_last
        total = _sublane_row(incl, SUBLANES - 1)
        totals[st, hd] = total

        seg_levels = []
        for c in range(TIME_BITS - SEG_BITS):
            upper8 = ((sub8 >> c) & 1) == 1
            incl_mid = jnp.zeros_like(incl)
            for s_mid in sorted({((s >> (c + 1)) << (c + 1)) | ((1 << c) - 1) for s in range(SUBLANES)}):
                in_block = (sub8 >> (c + 1)) == (s_mid >> (c + 1))
                incl_mid = jnp.where(in_block, _sublane_row(incl, s_mid), incl_mid)
            q_add = jnp.where(upper8, offset - incl_mid, MASKED_OUT)
            k_add = jnp.where(upper8, MASKED_OUT, incl_mid - incl + cum_last)
            seg_levels.append((twice(q_add), twice(k_add)))
        q_const = twice(offset)
        k_const = twice(cum_last + total - incl)

        for m in range(SEG // 2):
            rows = pl.ds(m * PAIR_ROWS, PAIR_ROWS)
            q = z_ref[pl.ds(base + m * PAIR_ROWS, PAIR_ROWS), q_l]
            k = k_b[rows, :]
            cum = cum_b[rows, :]
            for b in range(SEG_BITS):
                if b == 0:
                    dec = jnp.exp2(cum - twice(cum[0:SUBLANES, :]) + first_masked)
                    lvq[0, rows, :] = (q * dec).astype(BF16)
                    lvk[0, rows, :] = jnp.where(row16 < SUBLANES, k, 0.0).astype(BF16)
                    continue
                mid = ((m >> b) << (b + 1)) | ((1 << b) - 1)
                ref = twice(cum_b[pl.ds(mid * SUBLANES, SUBLANES), :])
                if (m >> (b - 1)) & 1:
                    lvq[b, rows, :] = (q * jnp.exp2(cum - ref)).astype(BF16)
                else:
                    lvk[b, rows, :] = (k * jnp.exp2(ref - cum)).astype(BF16)
            ncum = -cum
            for b in range(SEG_BITS, TIME_BITS):
                q_add, k_add = seg_levels[b - SEG_BITS]
                lvq[b, rows, :] = (q * jnp.exp2(cum + q_add)).astype(BF16)
                lvk[b, rows, :] = (k * jnp.exp2(ncum + k_add)).astype(BF16)
            qh_b[rows, :] = (q * jnp.exp2(cum + q_const)).astype(BF16)
            kh_b[rows, :] = (k * jnp.exp2(ncum + k_const)).astype(BF16)
        vb_b[...] = z_ref[blk_rows(st), v_l].astype(BF16)

    def head_products(st, hd):
        par = hd % 2
        k_b, lvq, lvk = k_ref.at[par], lvq_ref.at[par], lvk_ref.at[par]
        qh_b, kh_b, vb_b = qh_ref.at[par], kh_ref.at[par], vb_ref.at[par]
        q_l = pl.ds(q_c + hd * LANES, LANES)
        v_l = pl.ds(v_c + hd * LANES, LANES)
        att = jnp.zeros((PT, PT), BF16)
        for b in range(TIME_BITS):
            part = lax.dot_general(lvq[b], lvk[b], nt, preferred_element_type=F32)
            att = jnp.where(lvl_ref[...] == b, part.astype(BF16), att)

        state = state_ref[hd]
        diag = jnp.sum(z_ref[blk_rows(st), q_l] * k_b[...], axis=-1, keepdims=True)
        o = jnp.dot(att, vb_b[...], preferred_element_type=F32)
        o = o + lax.dot_general(qh_b[...], state.astype(BF16), nt, preferred_element_type=F32)
        oh_ref[hd, blk_rows(st), :] = o + diag * z_ref[blk_rows(st), v_l]
        upd = lax.dot_general(vb_b[...], kh_b[...], tn, preferred_element_type=F32)
        state_ref[hd] = state * jnp.exp2(totals[st, hd][0:1, :]) + upd

    def head_outputs(st, r):
        rows = pl.ds(st * PT + r * ROW_CHUNK, ROW_CHUNK)
        for hd in range(heads):
            o = oh_ref[hd, rows, :]
            o = o * lax.rsqrt(jnp.mean(o * o, axis=-1, keepdims=True) + EPS) * ng_ref[:, pl.ds(hd * LANES, LANES)]
            gd = z_ref[rows, pl.ds(gd_c + hd * LANES, LANES)]
            y_ref[rows, pl.ds(w + hd * LANES, LANES)] = (o * (gd * _sigmoid(gd))).astype(BF16)

    _rmsnorm_to(x_ref, g_ref, h_ref)
    z_ref[...] = jnp.dot(h_ref[...], win_ref[...], preferred_element_type=F32)
    for st in range(NSUB):
        pool(st)
        for hd in range(heads):
            head_factors(st, hd)
            head_products(st, hd)
        for r in range(CHUNKS_PER_BLOCK):
            head_outputs(st, r)
    o_ref[...] = x_ref[...] + jnp.dot(y_ref[...], wout_ref[...], preferred_element_type=F32)


def _odd_call(x, params, layer):
    bsz, seq, d = x.shape
    g, win, pw, pscale, lbl, ng, levels, wout = params
    zin = win.shape[-1]
    w = zin // 5
    heads = w // LANES
    hist = POOL_WINDOWS[-1] // 2
    o = layer // 2
    xspec = pl.BlockSpec((None, SEQ_TILE, d), lambda b, j: (b, j, 0))
    specs = [_layer_spec(g, layer), _layer_spec(win, o), _layer_spec(pw, o), _layer_spec(pscale, o),
             _whole_spec(lbl), _layer_spec(ng, o), _whole_spec(levels), _layer_spec(wout, o)]
    return pl.pallas_call(
        functools.partial(_odd_kernel, layer=o),
        out_shape=jax.ShapeDtypeStruct(x.shape, x.dtype),
        grid=(bsz, seq // SEQ_TILE),
        in_specs=[xspec] + specs,
        out_specs=xspec,
        scratch_shapes=[
            pltpu.VMEM((SEQ_TILE, d), BF16),
            pltpu.VMEM((SEQ_TILE, zin), F32),
            pltpu.VMEM((PT + hist * SUBLANES, w), F32),
            pltpu.VMEM((PT + hist * SUBLANES, w), F32),
            pltpu.VMEM(((POOL_WINDOWS[-1] - 1) * SUBLANES, w), F32),
            pltpu.VMEM((2, PT, LANES), F32),
            pltpu.VMEM((2, PT, LANES), F32),
            pltpu.VMEM((2, TIME_BITS, PT, LANES), BF16),
            pltpu.VMEM((2, TIME_BITS, PT, LANES), BF16),
            pltpu.VMEM((2, PT, LANES), BF16),
            pltpu.VMEM((2, PT, LANES), BF16),
            pltpu.VMEM((2, PT, LANES), BF16),
            pltpu.VMEM((heads, SEQ_TILE, LANES), F32),
            pltpu.VMEM((SEQ_TILE, 2 * w), BF16),
            pltpu.VMEM((heads, LANES, LANES), F32),
        ],
        compiler_params=_cparams(),
        name="pool_hgrn_mixer",
    )(x, *params)


def _cast_kernel(*refs):
    n = len(refs) // 2
    for x_ref, o_ref in zip(refs[:n], refs[n:]):
        o_ref[...] = x_ref[...].astype(o_ref.dtype)


def _to_bf16(*wstacks):
    shape = wstacks[0].shape
    assert all(w.shape == shape for w in wstacks)
    rows = int(np.prod(shape[:-1]))
    cols = shape[-1]
    assert rows % CAST_ROWS == 0
    spec = pl.BlockSpec((CAST_ROWS, cols), lambda i: (i, 0))
    outs = pl.pallas_call(
        _cast_kernel,
        out_shape=[jax.ShapeDtypeStruct((rows, cols), BF16)] * len(wstacks),
        grid=(rows // CAST_ROWS,),
        in_specs=[spec] * len(wstacks),
        out_specs=[spec] * len(wstacks),
        compiler_params=pltpu.CompilerParams(vmem_limit_bytes=VMEM_LIMIT_BYTES),
        name="weights_to_bf16",
    )(*[w.reshape(rows, cols) for w in wstacks])
    return [o.reshape(shape) for o in outs]


def _to_segment_major(x):
    b, s, d = x.shape
    return x.reshape(b, s // PT, SUBLANES, SEG, d).transpose(0, 1, 3, 2, 4).reshape(b, s, d)


def _from_segment_major(x):
    b, s, d = x.shape
    return x.reshape(b, s // PT, SEG, SUBLANES, d).transpose(0, 1, 3, 2, 4).reshape(b, s, d)


def _block_diag_tiles(wblocks):
    layers, nb, n, _ = wblocks.shape
    per = MXU_DIM // n
    grouped = wblocks.reshape(layers, nb // per, per, n, n)
    tiles = jnp.einsum("ab,ltaij->ltaibj", jnp.eye(per, dtype=wblocks.dtype), grouped)
    return tiles.reshape(layers, nb // per, MXU_DIM, MXU_DIM).astype(BF16)


def kernel(x, g_mix, g_ffn, g_final, w_in_even, w_out_even, lru_conv_w, lru_conv_b, lru_wa, lru_ba, lru_wi, lru_bi, lru_lambda, sconv_w, w_in_odd, w_out_odd, pool_w, pool_scale, hgrn_lb_logits, hgrn_norm_g, ffn_w_up, ffn_w_gate, ffn_conv_w, ffn_conv_b, ffn_w_down):
    depth = g_mix.shape[0]
    assert x.shape[1] % SEQ_TILE == 0
    rows = lambda v: v.reshape(v.shape[0], 1, -1).astype(F32)
    levels = jnp.asarray(_pair_levels(), dtype=BF16)
    if w_in_even.shape == w_in_odd.shape:
        win_even, win_odd = _to_bf16(w_in_even, w_in_odd)
        wout_even, wout_odd = _to_bf16(w_out_even, w_out_odd)
    else:
        (win_even,), (win_odd,) = _to_bf16(w_in_even), _to_bf16(w_in_odd)
        (wout_even,), (wout_odd,) = _to_bf16(w_out_even), _to_bf16(w_out_odd)
    wup, wgate = _to_bf16(ffn_w_up, ffn_w_gate)
    (wdown,) = _to_bf16(ffn_w_down)
    even = (rows(g_mix), win_even, lru_conv_w, rows(lru_conv_b), _block_diag_tiles(lru_wa),
            _block_diag_tiles(lru_wi), rows(lru_ba), rows(lru_bi), rows(lru_lambda), sconv_w, wout_even)
    odd = (rows(g_mix), win_odd, _block_diag_tiles(pool_w), rows(pool_scale),
           hgrn_lb_logits.astype(F32), rows(hgrn_norm_g), levels, wout_odd)
    ffn = (rows(g_ffn), wup, wgate, ffn_conv_w, rows(ffn_conv_b), wdown, g_final.reshape(1, -1).astype(F32))
    x = _to_segment_major(x)
    for l in range(depth):
        if l % 2 == 0:
            x = _even_call(x, even, l)
        else:
            x = _odd_call(x, odd, l)
        x = _ffn_call(x, ffn, l, l == depth - 1)
    return _from_segment_major(x)
```

```python
import functools

import jax
import jax.numpy as jnp
import numpy as np
from jax import lax
from jax.experimental import pallas as pl
from jax.experimental.pallas import tpu as pltpu

F32 = jnp.float32
BF16 = jnp.bfloat16

SUBLANES = 8
LANES = 128
MXU_DIM = 256
VMEM_LIMIT_BYTES = 56 * 1024 * 1024

EPS = 1e-6
LOG2_E = 1.4426950408889634
MASKED_OUT = -1e30
LRU_C = 8.0
POOL_WINDOWS = (2, 4, 8, 16)

SEQ_TILE = 512
FFN_TILE = 1024
PT = 256
NSUB = SEQ_TILE // PT
SEG = PT // SUBLANES
SEG_BITS = SEG.bit_length() - 1
TIME_BITS = PT.bit_length() - 1
ROW_CHUNK = 64
CHUNKS_PER_BLOCK = PT // ROW_CHUNK
FFN_CHUNK = 256
PAIR_ROWS = 2 * SUBLANES
CAST_ROWS = 512


def _cparams():
    return pltpu.CompilerParams(
        dimension_semantics=("arbitrary", "arbitrary"),
        vmem_limit_bytes=VMEM_LIMIT_BYTES,
    )


def _sigmoid(v):
    return 0.5 * jnp.tanh(0.5 * v) + 0.5


def _rmsnorm_rows(x, g):
    ms = jnp.mean(x * x, axis=-1, keepdims=True)
    return x * lax.rsqrt(ms + EPS) * g


def _rmsnorm_to(x_ref, g_ref, h_ref, row0=0, nrows=SEQ_TILE):
    for r in range(nrows // ROW_CHUNK):
        rows = pl.ds(row0 + r * ROW_CHUNK, ROW_CHUNK)
        h_ref[rows, :] = _rmsnorm_rows(x_ref[rows, :], g_ref[...]).astype(h_ref.dtype)


def _fill_history(buf_ref, base, carry_ref, groups, lanes, carry_lanes=None):
    carry_lanes = lanes if carry_lanes is None else carry_lanes
    for g in range(groups):
        tail = buf_ref[pl.ds(base + PT + g * SUBLANES, SUBLANES), lanes]
        rolled = pltpu.roll(tail, 1, axis=0)
        sub = lax.broadcasted_iota(jnp.int32, rolled.shape, 0)
        prev = carry_ref[pl.ds(g * SUBLANES, SUBLANES), carry_lanes]
        buf_ref[pl.ds(base + g * SUBLANES, SUBLANES), lanes] = jnp.where(sub == 0, prev, rolled)
        carry_ref[pl.ds(g * SUBLANES, SUBLANES), carry_lanes] = rolled


def _row_time(t0, shape):
    row = lax.broadcasted_iota(jnp.int32, shape, 0)
    sub = row & (SUBLANES - 1)
    grp = row >> (SUBLANES.bit_length() - 1)
    return t0 + sub * SEG + grp


def _layer_spec(a, l, single=False):
    return pl.BlockSpec((None,) + a.shape[1:], lambda b, j: (l,) + (0,) * (a.ndim - 1),
                        pipeline_mode=pl.Buffered(1) if single else None)


def _whole_spec(a):
    return pl.BlockSpec(a.shape, lambda b, j: (0,) * a.ndim)


def _ffn_kernel(x_ref, g_ref, wup_ref, wgate_ref, cw_ref, cb_ref, wdown_ref, gfin_ref, o_ref,
                h_ref, up_ref, gate_ref, act_ref, carry_ref, *, final_norm):
    hist = 2
    hrows = hist * SUBLANES
    stride = hrows + PT
    ffn_dim = wup_ref.shape[1]
    tile = x_ref.shape[0]
    nsub = tile // PT

    @pl.when(pl.program_id(1) == 0)
    def _():
        carry_ref[...] = jnp.zeros_like(carry_ref)

    _rmsnorm_to(x_ref, g_ref, h_ref, 0, tile)

    for c in range(ffn_dim // FFN_CHUNK):
        cols = pl.ds(c * FFN_CHUNK, FFN_CHUNK)
        up = up_ref.at[c % 2]
        gate = gate_ref.at[c % 2]
        proj = jnp.dot(h_ref[...], wup_ref[:, cols], preferred_element_type=F32)
        for st in range(nsub):
            up[pl.ds(st * stride + hrows, PT), :] = proj[st * PT:(st + 1) * PT, :]
        gate[...] = jnp.dot(h_ref[...], wgate_ref[:, cols], preferred_element_type=F32)
        w0 = cw_ref[0:1, cols]
        w1 = cw_ref[1:2, cols]
        w2 = cw_ref[2:3, cols]
        b = cb_ref[:, cols]
        for st in range(nsub):
            _fill_history(up, st * stride, carry_ref, hist, slice(None), cols)
            for r in range(CHUNKS_PER_BLOCK):
                r0 = st * stride + r * ROW_CHUNK
                u = (up[pl.ds(r0 + hrows, ROW_CHUNK), :] * w2
                     + up[pl.ds(r0 + hrows - SUBLANES, ROW_CHUNK), :] * w1
                     + up[pl.ds(r0, ROW_CHUNK), :] * w0 + b)
                rows = pl.ds(st * PT + r * ROW_CHUNK, ROW_CHUNK)
                act_ref[rows, cols] = (jax.nn.gelu(u) * gate[rows, :]).astype(BF16)

    o_ref[...] = x_ref[...] + jnp.dot(act_ref[...], wdown_ref[...], preferred_element_type=F32)
    if final_norm:
        for r in range(tile // ROW_CHUNK):
            rows = pl.ds(r * ROW_CHUNK, ROW_CHUNK)
            o_ref[rows, :] = _rmsnorm_rows(o_ref[rows, :], gfin_ref[...])


def _ffn_call(x, params, layer, final_norm):
    bsz, seq, d = x.shape
    g, wup, wgate, cw, cb, wdown, gfin = params
    f = wup.shape[-1]
    xspec = pl.BlockSpec((None, FFN_TILE, d), lambda b, j: (b, j, 0))
    return pl.pallas_call(
        functools.partial(_ffn_kernel, final_norm=final_norm),
        out_shape=jax.ShapeDtypeStruct(x.shape, x.dtype),
        grid=(bsz, seq // FFN_TILE),
        in_specs=[xspec] + [_layer_spec(a, layer, single=True) for a in params[:-1]] + [_whole_spec(gfin)],
        out_specs=xspec,
        scratch_shapes=[
            pltpu.VMEM((FFN_TILE, d), BF16),
            pltpu.VMEM((2, FFN_TILE // PT * (PT + 2 * SUBLANES), FFN_CHUNK), F32),
            pltpu.VMEM((2, FFN_TILE, FFN_CHUNK), F32),
            pltpu.VMEM((FFN_TILE, f), BF16),
            pltpu.VMEM((2 * SUBLANES, f), F32),
        ],
        compiler_params=_cparams(),
        name="conv_ffn",
    )(x, *params)


def _even_kernel(x_ref, g_ref, win_ref, cw_ref, cb_ref, wa_ref, wi_ref, ba_ref, bi_ref, lam_ref,
                 sw_ref, wout_ref, o_ref,
                 h_ref, *scratch):
    z_refs = scratch[:NSUB]
    xc_ref, xcb_ref, gate_ref, y_ref, zcarry_ref, hcarry_ref = scratch[NSUB:]
    hist = 3
    hrows = hist * SUBLANES
    w = xc_ref.shape[1]
    j = pl.program_id(1)

    @pl.when(j == 0)
    def _():
        zcarry_ref[...] = jnp.zeros_like(zcarry_ref)
        hcarry_ref[...] = jnp.zeros_like(hcarry_ref)

    xa_c, ga_c, hb_c, bg_c, cg_c = (pl.ds(k * w, w) for k in range(5))
    r_l, i_l = pl.ds(0, w), pl.ds(w, w)
    sub = lax.broadcasted_iota(jnp.int32, (SUBLANES, w), 0)

    lam = lam_ref[...]
    neg_c_sp = (-LRU_C * LOG2_E) * (jnp.maximum(-lam, 0.0) + jnp.log1p(jnp.exp(-jnp.abs(lam))))

    blk_rows = lambda st: pl.ds(st * PT, PT)
    chunk_rows = lambda st, r: pl.ds(st * PT + r * ROW_CHUNK, ROW_CHUNK)
    h_rep = {}

    def project(st, k):
        cols = pl.ds(k * w, w)
        z_refs[st][pl.ds(hrows, PT), cols] = jnp.dot(
            h_ref[blk_rows(st), :], win_ref[:, cols], preferred_element_type=F32)

    def history(st):
        for lanes in (xa_c, hb_c, cg_c):
            _fill_history(z_refs[st], 0, zcarry_ref, hist, lanes)

    def lru_conv(st, r):
        z_ref = z_refs[st]
        z0 = hrows + r * ROW_CHUNK
        acc = cb_ref[...] + z_ref[pl.ds(z0, ROW_CHUNK), xa_c] * cw_ref[3:4, :]
        for k in range(1, 4):
            acc = acc + z_ref[pl.ds(z0 - k * SUBLANES, ROW_CHUNK), xa_c] * cw_ref[3 - k:4 - k, :]
        xc_ref[chunk_rows(st, r), :] = acc
        xcb_ref[chunk_rows(st, r), :] = acc.astype(BF16)

    def gate_matmuls(st):
        for blk in range(w // MXU_DIM):
            lanes = pl.ds(blk * MXU_DIM, MXU_DIM)
            gate_ref[blk_rows(st), pl.ds(blk * MXU_DIM, MXU_DIM)] = jnp.dot(
                xcb_ref[blk_rows(st), lanes], wa_ref[blk], preferred_element_type=F32)
            gate_ref[blk_rows(st), pl.ds(w + blk * MXU_DIM, MXU_DIM)] = jnp.dot(
                xcb_ref[blk_rows(st), lanes], wi_ref[blk], preferred_element_type=F32)

    def gates(st, r):
        rows = chunk_rows(st, r)
        rg = _sigmoid(gate_ref[rows, r_l] + ba_ref[...])
        ig = _sigmoid(gate_ref[rows, i_l] + bi_ref[...])
        a = jnp.exp2(rg * neg_c_sp)
        mult = jnp.sqrt(1.0 - a * a)
        if st == 0 and r == 0:
            t = _row_time(j * SEQ_TILE, mult.shape)
            mult = jnp.where(t == 0, 1.0, mult)
        xc_ref[rows, :] = mult * ig * xc_ref[rows, :]
        gate_ref[rows, r_l] = a

    def scan(st):
        h = jnp.zeros((SUBLANES, w), F32)
        p = h + 1.0
        for i in range(SEG):
            rows = pl.ds(st * PT + i * SUBLANES, SUBLANES)
            a = gate_ref[rows, r_l]
            h = a * h + xc_ref[rows, :]
            p = a * p
            xc_ref[rows, :] = h
            gate_ref[rows, r_l] = p
        h_in = jnp.where(sub == 0, hcarry_ref[...], 0.0)
        for s in range(SUBLANES - 1):
            nxt = pltpu.roll(h + p * h_in, 1, axis=0)
            h_in = jnp.where(sub == s + 1, nxt, h_in)
        hcarry_ref[...] = pltpu.roll(h + p * h_in, 1, axis=0)
        h_rep[st] = jnp.concatenate([h_in] * (ROW_CHUNK // SUBLANES), axis=0)

    def outputs(st, r):
        z_ref = z_refs[st]
        rows = chunk_rows(st, r)
        z0 = hrows + r * ROW_CHUNK
        hs = xc_ref[rows, :] + gate_ref[rows, r_l] * h_rep[st]
        ya = hs * jax.nn.gelu(z_ref[pl.ds(z0, ROW_CHUNK), ga_c])
        conv = None
        for k in range(3):
            zrows = pl.ds(z0 - k * SUBLANES, ROW_CHUNK)
            term = z_ref[zrows, cg_c] * z_ref[zrows, hb_c] * sw_ref[2 - k:3 - k, :]
            conv = term if conv is None else conv + term
        yb = z_ref[pl.ds(z0, ROW_CHUNK), bg_c] * conv
        y_ref[rows, pl.ds(0, w)] = ya.astype(BF16)
        y_ref[rows, pl.ds(w, w)] = yb.astype(BF16)

    def out_project(st, n):
        cols = pl.ds(n * MXU_DIM, MXU_DIM)
        o_ref[blk_rows(st), cols] = x_ref[blk_rows(st), cols] + jnp.dot(
            y_ref[blk_rows(st), :], wout_ref[:, cols], preferred_element_type=F32)

    d_model = x_ref.shape[1]
    n_out = d_model // MXU_DIM
    _rmsnorm_to(x_ref, g_ref, h_ref, 0, PT)
    for k in range(5):
        project(0, k)
    for st in range(NSUB):
        nxt = st + 1 if st + 1 < NSUB else None
        prv = st - 1 if st > 0 else None
        pieces = []
        if nxt is not None:
            pieces += [functools.partial(project, nxt, k) for k in range(5)]
        if prv is not None:
            pieces += [functools.partial(out_project, prv, n) for n in range(n_out)]
        vector_steps = ([functools.partial(lru_conv, st, r) for r in range(CHUNKS_PER_BLOCK)]
                        + [functools.partial(gate_matmuls, st)]
                        + [functools.partial(gates, st, r) for r in range(CHUNKS_PER_BLOCK)]
                        + [functools.partial(scan, st)]
                        + [functools.partial(outputs, st, r) for r in range(CHUNKS_PER_BLOCK)])
        history(st)
        if nxt is not None:
            _rmsnorm_to(x_ref, g_ref, h_ref, nxt * PT, PT)
        emitted = 0
        for idx, step in enumerate(vector_steps):
            step()
            due = (idx + 1) * len(pieces) // len(vector_steps)
            while emitted < due:
                pieces[emitted]()
                emitted += 1
    for n in range(n_out):
        out_project(NSUB - 1, n)


def _even_call(x, params, layer):
    bsz, seq, d = x.shape
    zin = params[1].shape[-1]
    w = zin // 5
    xspec = pl.BlockSpec((None, SEQ_TILE, d), lambda b, j: (b, j, 0))
    specs = [_layer_spec(params[0], layer)] + [_layer_spec(a, layer // 2) for a in params[1:]]
    return pl.pallas_call(
        _even_kernel,
        out_shape=jax.ShapeDtypeStruct(x.shape, x.dtype),
        grid=(bsz, seq // SEQ_TILE),
        in_specs=[xspec] + specs,
        out_specs=xspec,
        scratch_shapes=[
            pltpu.VMEM((SEQ_TILE, d), BF16),
        ] + [pltpu.VMEM((PT + 3 * SUBLANES, zin), F32) for _ in range(NSUB)] + [
            pltpu.VMEM((SEQ_TILE, w), F32),
            pltpu.VMEM((SEQ_TILE, w), BF16),
            pltpu.VMEM((SEQ_TILE, 2 * w), F32),
            pltpu.VMEM((SEQ_TILE, 2 * w), BF16),
            pltpu.VMEM((3 * SUBLANES, zin), F32),
            pltpu.VMEM((SUBLANES, w), F32),
        ],
        compiler_params=_cparams(),
        name="lru_sconv_mixer",
    )(x, *params)


def _pair_levels():
    p = np.arange(PT)
    t = (p % SUBLANES) * SEG + p // SUBLANES
    x = t[:, None] ^ t[None, :]
    lvl = np.full(x.shape, -1.0, np.float32)
    for b in range(TIME_BITS):
        lvl[(x >> b) == 1] = b
    return lvl


def _sublane_row(v, s):
    return jnp.broadcast_to(v[s:s + 1, :], v.shape)


def _odd_kernel(x_ref, g_ref, win_ref, pw_ref, pscale_ref, lbl_ref, ng_ref, lvl_ref, wout_ref, o_ref,
                h_ref, z_ref, pa_ref, pb_ref, pcarry_ref, cum_ref, k_ref, lvq_ref, lvk_ref, qh_ref, kh_ref,
                vb_ref, oh_ref, y_ref, state_ref, *, layer):
    w = pscale_ref.shape[1]
    heads = w // LANES
    j = pl.program_id(1)
    hist = POOL_WINDOWS[-1] // 2
    hrows = hist * SUBLANES

    @pl.when(jnp.logical_and(pl.program_id(0) == 0, j == 0))
    def _():
        lvq_ref[...] = jnp.zeros_like(lvq_ref)
        lvk_ref[...] = jnp.zeros_like(lvk_ref)

    @pl.when(j == 0)
    def _():
        pcarry_ref[...] = jnp.zeros_like(pcarry_ref)
        state_ref[...] = jnp.zeros_like(state_ref)

    uc_c, q_c, fz_c, v_c, gd_c = (k * w for k in range(5))

    logits = lbl_ref[...]
    sm = jnp.exp(logits - jnp.max(logits, axis=0, keepdims=True))
    sm = sm / jnp.sum(sm, axis=0, keepdims=True)
    lb = jnp.zeros((1, w), F32)
    for o in range(1, layer + 1):
        lb = lb + sm[o:o + 1, :]
    log_1m_lb = jnp.log1p(-lb)

    sub8 = lax.broadcasted_iota(jnp.int32, (SUBLANES, LANES), 0)
    row16 = lax.broadcasted_iota(jnp.int32, (PAIR_ROWS, LANES), 0)
    first_masked = jnp.where(row16 < SUBLANES, MASKED_OUT, 0.0)
    nt = (((1,), (1,)), ((), ()))
    tn = (((0,), (0,)), ((), ()))
    twice = lambda v: jnp.concatenate([v, v], axis=0)
    bufs = (pa_ref, pb_ref)

    blk_rows = lambda st: pl.ds(st * PT, PT)

    def pool(st):
        base = st * PT
        pa_ref[pl.ds(hrows, PT), :] = z_ref[blk_rows(st), pl.ds(uc_c, w)]
        for stage in range(len(POOL_WINDOWS)):
            shift = 1 << stage
            src = bufs[stage % 2]
            dst = bufs[(stage + 1) % 2]
            lanes = pl.ds(stage * LANES, w - stage * LANES)
            for gidx in range(shift):
                tail = src[pl.ds(hrows + PT - (shift - gidx) * SUBLANES, SUBLANES), lanes]
                rolled = pltpu.roll(tail, 1, axis=0)
                crow = pl.ds((shift - 1 + gidx) * SUBLANES, SUBLANES)
                prev = pcarry_ref[crow, lanes]
                src[pl.ds(hrows - (shift - gidx) * SUBLANES, SUBLANES), lanes] = jnp.where(
                    lax.broadcasted_iota(jnp.int32, rolled.shape, 0) == 0, prev, rolled)
                pcarry_ref[crow, lanes] = rolled
            for r in range(CHUNKS_PER_BLOCK):
                r0 = r * ROW_CHUNK + hrows
                dst[pl.ds(r0, ROW_CHUNK), lanes] = (
                    src[pl.ds(r0, ROW_CHUNK), lanes] + src[pl.ds(r0 - shift * SUBLANES, ROW_CHUNK), lanes])

        for r in range(CHUNKS_PER_BLOCK):
            short = st == 0 and r * (ROW_CHUNK // SUBLANES) < POOL_WINDOWS[-1]
            t = _row_time(j * SEQ_TILE + r * (ROW_CHUNK // SUBLANES), (ROW_CHUNK, LANES)) if short else None
            for gi, win in enumerate(POOL_WINDOWS):
                buf = bufs[(gi + 1) % 2]
                lanes = pl.ds(gi * LANES, LANES)
                rows = pl.ds(base + r * ROW_CHUNK, ROW_CHUNK)
                total = buf[pl.ds(hrows + r * ROW_CHUNK, ROW_CHUNK), lanes]
                mean = total / jnp.minimum(t + 1, win).astype(F32) if short else total * (1.0 / win)
                y_ref[rows, lanes] = (mean - z_ref[rows, pl.ds(uc_c + gi * LANES, LANES)]).astype(BF16)

        for blk in range(w // MXU_DIM):
            lanes = pl.ds(blk * MXU_DIM, MXU_DIM)
            mixed = jnp.dot(y_ref[blk_rows(st), lanes], pw_ref[blk], preferred_element_type=F32)
            y_ref[blk_rows(st), lanes] = (mixed * pscale_ref[:, lanes]).astype(BF16)

    totals = {}

    def head_factors(st, hd):
        base = st * PT
        par = hd % 2
        cum_b, k_b, lvq, lvk = cum_ref.at[par], k_ref.at[par], lvq_ref.at[par], lvk_ref.at[par]
        qh_b, kh_b, vb_b = qh_ref.at[par], kh_ref.at[par], vb_ref.at[par]
        lbh = lb[:, hd * LANES:(hd + 1) * LANES]
        l1 = log_1m_lb[:, hd * LANES:(hd + 1) * LANES]
        om = 1.0 - lbh
        q_l = pl.ds(q_c + hd * LANES, LANES)
        fz_l = pl.ds(fz_c + hd * LANES, LANES)
        v_l = pl.ds(v_c + hd * LANES, LANES)

        cum = jnp.zeros((SUBLANES, LANES), F32)
        for r in range(CHUNKS_PER_BLOCK):
            fz = z_ref[pl.ds(base + r * ROW_CHUNK, ROW_CHUNK), fz_l]
            e = jnp.exp(-jnp.abs(fz))
            rcp = 1.0 / (1.0 + e)
            er = e * rcp
            pos = fz >= 0
            f = lbh + om * jnp.where(pos, rcp, er)
            log_f = LOG2_E * jnp.maximum(jnp.log(f), l1 + (jnp.minimum(fz, 0.0) - e))
            k_b[pl.ds(r * ROW_CHUNK, ROW_CHUNK), :] = om * jnp.where(pos, er, rcp)
            parts = []
            for g in range(ROW_CHUNK // SUBLANES):
                cum = cum + log_f[g * SUBLANES:(g + 1) * SUBLANES, :]
                parts.append(cum)
            cum_b[pl.ds(r * ROW_CHUNK, ROW_CHUNK), :] = jnp.concatenate(parts, axis=0)
        cum_last = cum

        incl = cum_last
        for sh in (1, 2, 4):
            incl = incl + jnp.where(sub8 >= sh, pltpu.roll(incl, sh, axis=0), 0.0)
        offset = incl - cum_last
        total = _sublane_row(incl, SUBLANES - 1)
        totals[st, hd] = total

        seg_levels = []
        for c in range(TIME_BITS - SEG_BITS):
            upper8 = ((sub8 >> c) & 1) == 1
            incl_mid = jnp.zeros_like(incl)
            for s_mid in sorted({((s >> (c + 1)) << (c + 1)) | ((1 << c) - 1) for s in range(SUBLANES)}):
                in_block = (sub8 >> (c + 1)) == (s_mid >> (c + 1))
                incl_mid = jnp.where(in_block, _sublane_row(incl, s_mid), incl_mid)
            q_add = jnp.where(upper8, offset - incl_mid, MASKED_OUT)
            k_add = jnp.where(upper8, MASKED_OUT, incl_mid - incl + cum_last)
            seg_levels.append((twice(q_add), twice(k_add)))
        q_const = twice(offset)
        k_const = twice(cum_last + total - incl)

        for m in range(SEG // 2):
            rows = pl.ds(m * PAIR_ROWS, PAIR_ROWS)
            q = z_ref[pl.ds(base + m * PAIR_ROWS, PAIR_ROWS), q_l]
            k = k_b[rows, :]
            cum = cum_b[rows, :]
            for b in range(SEG_BITS):
                if b == 0:
                    dec = jnp.exp2(cum - twice(cum[0:SUBLANES, :]) + first_masked)
                    lvq[0, rows, :] = (q * dec).astype(BF16)
                    lvk[0, rows, :] = jnp.where(row16 < SUBLANES, k, 0.0).astype(BF16)
                    continue
                mid = ((m >> b) << (b + 1)) | ((1 << b) - 1)
                ref = twice(cum_b[pl.ds(mid * SUBLANES, SUBLANES), :])
                if (m >> (b - 1)) & 1:
                    lvq[b, rows, :] = (q * jnp.exp2(cum - ref)).astype(BF16)
                else:
                    lvk[b, rows, :] = (k * jnp.exp2(ref - cum)).astype(BF16)
            ncum = -cum
            for b in range(SEG_BITS, TIME_BITS):
                q_add, k_add = seg_levels[b - SEG_BITS]
                lvq[b, rows, :] = (q * jnp.exp2(cum + q_add)).astype(BF16)
                lvk[b, rows, :] = (k * jnp.exp2(ncum + k_add)).astype(BF16)
            qh_b[rows, :] = (q * jnp.exp2(cum + q_const)).astype(BF16)
            kh_b[rows, :] = (k * jnp.exp2(ncum + k_const)).astype(BF16)
        vb_b[...] = z_ref[blk_rows(st), v_l].astype(BF16)

    def head_products(st, hd):
        par = hd % 2
        k_b, lvq, lvk = k_ref.at[par], lvq_ref.at[par], lvk_ref.at[par]
        qh_b, kh_b, vb_b = qh_ref.at[par], kh_ref.at[par], vb_ref.at[par]
        q_l = pl.ds(q_c + hd * LANES, LANES)
        v_l = pl.ds(v_c + hd * LANES, LANES)
        att = jnp.zeros((PT, PT), BF16)
        for b in range(TIME_BITS):
            part = lax.dot_general(lvq[b], lvk[b], nt, preferred_element_type=F32)
            att = jnp.where(lvl_ref[...] == b, part.astype(BF16), att)

        state = state_ref[hd]
        diag = jnp.sum(z_ref[blk_rows(st), q_l] * k_b[...], axis=-1, keepdims=True)
        o = jnp.dot(att, vb_b[...], preferred_element_type=F32)
        o = o + lax.dot_general(qh_b[...], state.astype(BF16), nt, preferred_element_type=F32)
        oh_ref[hd, blk_rows(st), :] = o + diag * z_ref[blk_rows(st), v_l]
        upd = lax.dot_general(vb_b[...], kh_b[...], tn, preferred_element_type=F32)
        state_ref[hd] = state * jnp.exp2(totals[st, hd][0:1, :]) + upd

    def head_outputs(st, r):
        rows = pl.ds(st * PT + r * ROW_CHUNK, ROW_CHUNK)
        for hd in range(heads):
            o = oh_ref[hd, rows, :]
            o = o * lax.rsqrt(jnp.mean(o * o, axis=-1, keepdims=True) + EPS) * ng_ref[:, pl.ds(hd * LANES, LANES)]
            gd = z_ref[rows, pl.ds(gd_c + hd * LANES, LANES)]
            y_ref[rows, pl.ds(w + hd * LANES, LANES)] = (o * (gd * _sigmoid(gd))).astype(BF16)

    _rmsnorm_to(x_ref, g_ref, h_ref)
    z_ref[...] = jnp.dot(h_ref[...], win_ref[...], preferred_element_type=F32)
    for st in range(NSUB):
        pool(st)
        for hd in range(heads):
            head_factors(st, hd)
            head_products(st, hd)
        for r in range(CHUNKS_PER_BLOCK):
            head_outputs(st, r)
    o_ref[...] = x_ref[...] + jnp.dot(y_ref[...], wout_ref[...], preferred_element_type=F32)


def _odd_call(x, params, layer):
    bsz, seq, d = x.shape
    g, win, pw, pscale, lbl, ng, levels, wout = params
    zin = win.shape[-1]
    w = zin // 5
    heads = w // LANES
    hist = POOL_WINDOWS[-1] // 2
    o = layer // 2
    xspec = pl.BlockSpec((None, SEQ_TILE, d), lambda b, j: (b, j, 0))
    specs = [_layer_spec(g, layer), _layer_spec(win, o), _layer_spec(pw, o), _layer_spec(pscale, o),
             _whole_spec(lbl), _layer_spec(ng, o), _whole_spec(levels), _layer_spec(wout, o)]
    return pl.pallas_call(
        functools.partial(_odd_kernel, layer=o),
        out_shape=jax.ShapeDtypeStruct(x.shape, x.dtype),
        grid=(bsz, seq // SEQ_TILE),
        in_specs=[xspec] + specs,
        out_specs=xspec,
        scratch_shapes=[
            pltpu.VMEM((SEQ_TILE, d), BF16),
            pltpu.VMEM((SEQ_TILE, zin), F32),
            pltpu.VMEM((PT + hist * SUBLANES, w), F32),
            pltpu.VMEM((PT + hist * SUBLANES, w), F32),
            pltpu.VMEM(((POOL_WINDOWS[-1] - 1) * SUBLANES, w), F32),
            pltpu.VMEM((2, PT, LANES), F32),
            pltpu.VMEM((2, PT, LANES), F32),
            pltpu.VMEM((2, TIME_BITS, PT, LANES), BF16),
            pltpu.VMEM((2, TIME_BITS, PT, LANES), BF16),
            pltpu.VMEM((2, PT, LANES), BF16),
            pltpu.VMEM((2, PT, LANES), BF16),
            pltpu.VMEM((2, PT, LANES), BF16),
            pltpu.VMEM((heads, SEQ_TILE, LANES), F32),
            pltpu.VMEM((SEQ_TILE, 2 * w), BF16),
            pltpu.VMEM((heads, LANES, LANES), F32),
        ],
        compiler_params=_cparams(),
        name="pool_hgrn_mixer",
    )(x, *params)


def _cast_kernel(*refs):
    n = len(refs) // 2
    for x_ref, o_ref in zip(refs[:n], refs[n:]):
        o_ref[...] = x_ref[...].astype(o_ref.dtype)


def _to_bf16(*wstacks):
    shape = wstacks[0].shape
    assert all(w.shape == shape for w in wstacks)
    rows = int(np.prod(shape[:-1]))
    cols = shape[-1]
    assert rows % CAST_ROWS == 0
    spec = pl.BlockSpec((CAST_ROWS, cols), lambda i: (i, 0))
    outs = pl.pallas_call(
        _cast_kernel,
        out_shape=[jax.ShapeDtypeStruct((rows, cols), BF16)] * len(wstacks),
        grid=(rows // CAST_ROWS,),
        in_specs=[spec] * len(wstacks),
        out_specs=[spec] * len(wstacks),
        compiler_params=pltpu.CompilerParams(vmem_limit_bytes=VMEM_LIMIT_BYTES),
        name="weights_to_bf16",
    )(*[w.reshape(rows, cols) for w in wstacks])
    return [o.reshape(shape) for o in outs]


def _to_segment_major(x):
    b, s, d = x.shape
    return x.reshape(b, s // PT, SUBLANES, SEG, d).transpose(0, 1, 3, 2, 4).reshape(b, s, d)


def _from_segment_major(x):
    b, s, d = x.shape
    return x.reshape(b, s // PT, SEG, SUBLANES, d).transpose(0, 1, 3, 2, 4).reshape(b, s, d)


def _block_diag_tiles(wblocks):
    layers, nb, n, _ = wblocks.shape
    per = MXU_DIM // n
    grouped = wblocks.reshape(layers, nb // per, per, n, n)
    tiles = jnp.einsum("ab,ltaij->ltaibj", jnp.eye(per, dtype=wblocks.dtype), grouped)
    return tiles.reshape(layers, nb // per, MXU_DIM, MXU_DIM).astype(BF16)


def kernel(x, g_mix, g_ffn, g_final, w_in_even, w_out_even, lru_conv_w, lru_conv_b, lru_wa, lru_ba, lru_wi, lru_bi, lru_lambda, sconv_w, w_in_odd, w_out_odd, pool_w, pool_scale, hgrn_lb_logits, hgrn_norm_g, ffn_w_up, ffn_w_gate, ffn_conv_w, ffn_conv_b, ffn_w_down):
    depth = g_mix.shape[0]
    assert x.shape[1] % SEQ_TILE == 0 and x.shape[1] % FFN_TILE == 0
    rows = lambda v: v.reshape(v.shape[0], 1, -1).astype(F32)
    levels = jnp.asarray(_pair_levels(), dtype=BF16)
    if w_in_even.shape == w_in_odd.shape:
        win_even, win_odd = _to_bf16(w_in_even, w_in_odd)
        wout_even, wout_odd = _to_bf16(w_out_even, w_out_odd)
    else:
        (win_even,), (win_odd,) = _to_bf16(w_in_even), _to_bf16(w_in_odd)
        (wout_even,), (wout_odd,) = _to_bf16(w_out_even), _to_bf16(w_out_odd)
    wup, wgate = _to_bf16(ffn_w_up, ffn_w_gate)
    (wdown,) = _to_bf16(ffn_w_down)
    even = (rows(g_mix), win_even, lru_conv_w, rows(lru_conv_b), _block_diag_tiles(lru_wa),
            _block_diag_tiles(lru_wi), rows(lru_ba), rows(lru_bi), rows(lru_lambda), sconv_w, wout_even)
    odd = (rows(g_mix), win_odd, _block_diag_tiles(pool_w), rows(pool_scale),
           hgrn_lb_logits.astype(F32), rows(hgrn_norm_g), levels, wout_odd)
    ffn = (rows(g_ffn), wup, wgate, ffn_conv_w, rows(ffn_conv_b), wdown, g_final.reshape(1, -1).astype(F32))
    x = _to_segment_major(x)
    for l in range(depth):
        if l % 2 == 0:
            x = _even_call(x, even, l)
        else:
            x = _odd_call(x, odd, l)
        x = _ffn_call(x, ffn, l, l == depth - 1)
    return _from_segment_major(x)
```
